```python
import jax, jax.numpy as jnp
from jax import lax
import numpy as np

D_MODEL = 4096
BATCH = 8
SEQ = 2048
DEPTH = 2

HEAD_DIM = 128
D_LRU = D_MODEL // 4
LRU_BLOCKS = D_LRU // HEAD_DIM
LRU_BLOCK = D_LRU // LRU_BLOCKS
CONV_WIDTH = 4
LRU_C = 8.0
N_FOX = D_MODEL // 4 // HEAD_DIM
D_FOX = N_FOX * HEAD_DIM
D_DIL = D_MODEL - D_LRU - D_FOX
N_DIL = D_DIL // HEAD_DIM
D_MIX = D_LRU + D_FOX + D_DIL
D_IN = 2 * D_LRU + 3 * D_FOX + N_FOX + 3 * D_DIL
D_FF = 11008
ROPE_THETA = 500000.0
ROPE_DIM = HEAD_DIM // 4
DILATED_PATTERNS = ((128, 1), (512, 4), (2048, 16))
Q_BLOCK = 128
EPS = 1e-6

kernel_name = "hymba_style_lru_fox_dilated_macaron"


def rmsnorm(x, g):
    x32 = x.astype(jnp.float32)
    y = x32 * lax.rsqrt(jnp.mean(x32 * x32, axis=-1, keepdims=True) + EPS)
    return (y * g.astype(jnp.float32)).astype(x.dtype)


def swiglu(x, w_in, w_out):
    gu = x @ w_in
    gate, up = gu[..., :D_FF], gu[..., D_FF:]
    return (jax.nn.silu(gate) * up) @ w_out


def rope_tables(seq):
    inv = 1.0 / (ROPE_THETA ** (jnp.arange(0, ROPE_DIM, 2, dtype=jnp.float32) / ROPE_DIM))
    ang = jnp.arange(seq, dtype=jnp.float32)[:, None] * inv[None, :]
    return jnp.cos(ang), jnp.sin(ang)


def apply_partial_rope(x, cos, sin):
    half = ROPE_DIM // 2
    x1 = x[..., :half].astype(jnp.float32)
    x2 = x[..., half:ROPE_DIM].astype(jnp.float32)
    c = cos[None, :, None, :]
    s = sin[None, :, None, :]
    rot = jnp.concatenate([x1 * c - x2 * s, x2 * c + x1 * s], axis=-1).astype(x.dtype)
    return jnp.concatenate([rot, x[..., ROPE_DIM:]], axis=-1)


def causal_depthwise_conv(x, w, b):
    S = x.shape[1]
    xp = jnp.pad(x, ((0, 0), (CONV_WIDTH - 1, 0), (0, 0)))
    y = b
    for j in range(CONV_WIDTH):
        y = y + xp[:, j:j + S] * w[j]
    return y


def rg_lru(x, w_a, b_a, w_x, b_x, lam):
    B, S, C = x.shape
    xb = x.reshape(B, S, LRU_BLOCKS, LRU_BLOCK)
    r = jax.nn.sigmoid(jnp.einsum('bsgi,gij->bsgj', xb, w_a).reshape(B, S, C) + b_a)
    i = jax.nn.sigmoid(jnp.einsum('bsgi,gij->bsgj', xb, w_x).reshape(B, S, C) + b_x)
    log_a = -LRU_C * r.astype(jnp.float32) * jax.nn.softplus(-lam.astype(jnp.float32))
    a = jnp.exp(log_a)
    u = jnp.sqrt(-jnp.expm1(2.0 * log_a)) * (i * x).astype(jnp.float32)

    def combine(c1, c2):
        a1, b1 = c1
        a2, b2 = c2
        return a1 * a2, a2 * b1 + b2

    _, h = lax.associative_scan(combine, (a, u), axis=1)
    return h.astype(x.dtype)


def forgetting_attention(q, k, v, log_f):
    B, S, H, hd = q.shape
    nb = S // Q_BLOCK
    cum = jnp.cumsum(log_f, axis=1).transpose(0, 2, 1)
    qb = q.reshape(B, nb, Q_BLOCK, H, hd).transpose(1, 0, 2, 3, 4)
    cb = cum.reshape(B, H, nb, Q_BLOCK).transpose(2, 0, 1, 3)
    k_pos = jnp.arange(S)
    scale = hd ** -0.5

    def one_block(args):
        blk, q_i, c_i = args
        s = jnp.einsum('bqhe,bkhe->bhqk', q_i, k).astype(jnp.float32) * scale
        s = s + c_i[..., None] - cum[:, :, None, :]
        q_pos = blk * Q_BLOCK + jnp.arange(Q_BLOCK)
        s = jnp.where((k_pos[None, :] <= q_pos[:, None])[None, None], s, -jnp.inf)
        p = jax.nn.softmax(s, axis=-1)
        return jnp.einsum('bhqk,bkhe->bqhe', p.astype(v.dtype), v)

    out = lax.map(one_block, (jnp.arange(nb), qb, cb))
    return out.transpose(1, 0, 2, 3, 4).reshape(B, S, H, hd)


def dilated_window_attention(q, k, v, window, dilation):
    B, S, H, hd = q.shape
    steps = window // dilation
    L = S // dilation
    nb = -(-L // Q_BLOCK)
    Lp = nb * Q_BLOCK

    def to_strided(t):
        t = t.reshape(B, L, dilation, H, hd).transpose(0, 2, 1, 3, 4)
        return jnp.pad(t, ((0, 0), (0, 0), (0, Lp - L), (0, 0), (0, 0)))

    def key_band(t):
        t = jnp.pad(to_strided(t), ((0, 0), (0, 0), (Q_BLOCK, 0), (0, 0), (0, 0)))
        t = t.reshape(B, dilation, nb + 1, Q_BLOCK, H, hd)
        return jnp.concatenate([t[:, :, :-1], t[:, :, 1:]], axis=3)

    qs = to_strided(q).reshape(B, dilation, nb, Q_BLOCK, H, hd)
    kw = key_band(k)
    vw = key_band(v)
    s = jnp.einsum('brnqhe,brnkhe->brnhqk', qs, kw).astype(jnp.float32) * (hd ** -0.5)
    blk = jnp.arange(nb)[:, None, None]
    qpos = jnp.arange(Q_BLOCK)[None, :, None]
    kpos = jnp.arange(2 * Q_BLOCK)[None, None, :] - Q_BLOCK
    rel = qpos - kpos
    valid = (rel >= 0) & (rel <= steps) & (blk * Q_BLOCK + kpos >= 0)
    s = jnp.where(valid[None, None, :, None], s, -jnp.inf)
    m = jnp.max(s, axis=-1, keepdims=True)
    e = jnp.exp(s - m)
    den = jnp.sum(e, axis=-1, keepdims=True)
    p = e / den
    lse = (m + jnp.log(den))[..., 0]
    o = jnp.einsum('brnhqk,brnkhe->brnqhe', p.astype(v.dtype), vw)
    o = o.reshape(B, dilation, Lp, H, hd)[:, :, :L].transpose(0, 2, 1, 3, 4).reshape(B, S, H, hd)
    lse = lse.transpose(0, 1, 2, 4, 3).reshape(B, dilation, Lp, H)[:, :, :L]
    lse = lse.transpose(0, 2, 1, 3).reshape(B, S, H)
    return o, lse


def dilated_mixture(q, k, v):
    outs, lses = [], []
    for window, dilation in DILATED_PATTERNS:
        o, l = dilated_window_attention(q, k, v, window, dilation)
        outs.append(o)
        lses.append(l)
    w = jax.nn.softmax(jnp.stack(lses, axis=0), axis=0)
    out = sum(w[i][..., None] * outs[i].astype(jnp.float32) for i in range(len(outs)))
    return out.astype(q.dtype)


def hybrid_mixer(h, w_in, conv_w, conv_b, lru_w_a, lru_b_a, lru_w_x, lru_b_x, lru_lam,
                 fox_b_f, out_norm_lru, out_norm_fox, out_norm_dil, w_out, cos, sin):
    B, S, _ = h.shape
    proj = h @ w_in
    sizes = (D_LRU, D_LRU, D_FOX, D_FOX, D_FOX, N_FOX, D_DIL, D_DIL, D_DIL)
    points = np.cumsum(sizes)[:-1].tolist()
    xa, ga, qf, kf, vf, ff, qd, kd, vd = jnp.split(proj, points, axis=-1)

    xa = causal_depthwise_conv(xa, conv_w, conv_b)
    y_a = rg_lru(xa, lru_w_a, lru_b_a, lru_w_x, lru_b_x, lru_lam) * jax.nn.gelu(ga)

    log_f = jax.nn.log_sigmoid(ff.astype(jnp.float32) + fox_b_f.astype(jnp.float32))
    hd4 = lambda t, n: t.reshape(B, S, n, HEAD_DIM)
    y_b = forgetting_attention(hd4(qf, N_FOX), hd4(kf, N_FOX), hd4(vf, N_FOX), log_f).reshape(B, S, D_FOX)

    qd = apply_partial_rope(hd4(qd, N_DIL), cos, sin)
    kd = apply_partial_rope(hd4(kd, N_DIL), cos, sin)
    y_c = dilated_mixture(qd, kd, hd4(vd, N_DIL)).reshape(B, S, D_DIL)

    y = jnp.concatenate([rmsnorm(y_a, out_norm_lru), rmsnorm(y_b, out_norm_fox), rmsnorm(y_c, out_norm_dil)], axis=-1)
    return y @ w_out


def setup_inputs(seed: int = 0) -> dict:
    key = jax.random.key(seed)
    ks = jax.random.split(key, 24)
    L = DEPTH

    def nrm(k, shape, fan_in):
        return jax.random.normal(k, shape, jnp.float32) * (fan_in ** -0.5)

    def gain(k, shape):
        return 1.0 + 0.05 * jax.random.normal(k, shape, jnp.float32)

    def bias(k, shape):
        return 0.01 * jax.random.normal(k, shape, jnp.float32)

    u = jax.random.uniform(ks[12], (L, D_LRU), jnp.float32, minval=0.9, maxval=0.999)
    s = u ** (1.0 / LRU_C)
    lru_lam = jnp.log(s) - jnp.log1p(-s)
    return {
        "x": jax.random.normal(ks[0], (BATCH, SEQ, D_MODEL), jnp.float32),
        "ffn1_norm": gain(ks[1], (L, D_MODEL)),
        "ffn1_w_in": nrm(ks[2], (L, D_MODEL, 2 * D_FF), D_MODEL),
        "ffn1_w_out": nrm(ks[3], (L, D_FF, D_MODEL), D_FF),
        "mix_norm": gain(ks[4], (L, D_MODEL)),
        "mix_w_in": nrm(ks[5], (L, D_MODEL, D_IN), D_MODEL),
        "conv_w": nrm(ks[6], (L, CONV_WIDTH, D_LRU), CONV_WIDTH),
        "conv_b": bias(ks[7], (L, D_LRU)),
        "lru_w_a": nrm(ks[8], (L, LRU_BLOCKS, LRU_BLOCK, LRU_BLOCK), LRU_BLOCK),
        "lru_b_a": bias(ks[9], (L, D_LRU)),
        "lru_w_x": nrm(ks[10], (L, LRU_BLOCKS, LRU_BLOCK, LRU_BLOCK), LRU_BLOCK),
        "lru_b_x": bias(ks[11], (L, D_LRU)),
        "lru_lam": lru_lam,
        "fox_b_f": 3.0 + 0.1 * jax.random.normal(ks[13], (L, N_FOX), jnp.float32),
        "out_norm_lru": gain(ks[14], (L, D_LRU)),
        "out_norm_fox": gain(ks[15], (L, D_FOX)),
        "out_norm_dil": gain(ks[16], (L, D_DIL)),
        "mix_w_out": nrm(ks[17], (L, D_MIX, D_MODEL), D_MIX),
        "ffn2_norm": gain(ks[18], (L, D_MODEL)),
        "ffn2_w_in": nrm(ks[19], (L, D_MODEL, 2 * D_FF), D_MODEL),
        "ffn2_w_out": nrm(ks[20], (L, D_FF, D_MODEL), D_FF),
        "final_norm": gain(ks[21], (D_MODEL,)),
    }


def reference(x, ffn1_norm, ffn1_w_in, ffn1_w_out, mix_norm, mix_w_in, conv_w, conv_b,
              lru_w_a, lru_b_a, lru_w_x, lru_b_x, lru_lam, fox_b_f,
              out_norm_lru, out_norm_fox, out_norm_dil, mix_w_out,
              ffn2_norm, ffn2_w_in, ffn2_w_out, final_norm):
    cos, sin = rope_tables(x.shape[1])
    h = x
    for l in range(DEPTH):
        h = h + 0.5 * swiglu(rmsnorm(h, ffn1_norm[l]), ffn1_w_in[l], ffn1_w_out[l])
        h = h + hybrid_mixer(rmsnorm(h, mix_norm[l]), mix_w_in[l], conv_w[l], conv_b[l],
                             lru_w_a[l], lru_b_a[l], lru_w_x[l], lru_b_x[l], lru_lam[l], fox_b_f[l],
                             out_norm_lru[l], out_norm_fox[l], out_norm_dil[l], mix_w_out[l], cos, sin)
        h = h + 0.5 * swiglu(rmsnorm(h, ffn2_norm[l]), ffn2_w_in[l], ffn2_w_out[l])
    return rmsnorm(h, final_norm)
```

```python
import functools

import numpy as np
import jax
import jax.numpy as jnp
from jax import lax
from jax.experimental import pallas as pl
from jax.experimental.pallas import tpu as pltpu

F32 = jnp.float32
BF16 = jnp.bfloat16

HEAD_DIM = 128
LRU_BLOCK = 128
CONV_WIDTH = 4
LRU_C = 8.0
ROPE_THETA = 500000.0
ROPE_DIM = HEAD_DIM // 4
DILATED_PATTERNS = ((128, 1), (512, 4), (2048, 16))
EPS = 1e-6
NEG_BIG = -1e30

VMEM_LIMIT_BYTES = 56 * 1024 * 1024


def _params(*sem):
    return pltpu.CompilerParams(dimension_semantics=sem, vmem_limit_bytes=VMEM_LIMIT_BYTES)


def _rmsnorm_kernel(x_ref, g_ref, o_ref):
    x = x_ref[...].astype(F32)
    ms = jnp.mean(x * x, axis=-1, keepdims=True)
    o_ref[...] = (x * lax.rsqrt(ms + EPS) * g_ref[...]).astype(o_ref.dtype)


def rmsnorm(x, g, out_dtype, tm=512):
    T, C = x.shape
    return pl.pallas_call(
        _rmsnorm_kernel,
        grid=(T // tm,),
        in_specs=[pl.BlockSpec((tm, C), lambda i: (i, 0)),
                  pl.BlockSpec((1, C), lambda i: (0, 0))],
        out_specs=pl.BlockSpec((tm, C), lambda i: (i, 0)),
        out_shape=jax.ShapeDtypeStruct((T, C), out_dtype),
        compiler_params=_params("parallel"),
        name="rmsnorm",
    )(x, g.reshape(1, C).astype(F32))


def _swiglu_in_kernel(x_ref, wg_ref, wu_ref, o_ref):
    x = x_ref[...]
    g = jnp.dot(x, wg_ref[...], preferred_element_type=F32)
    u = jnp.dot(x, wu_ref[...], preferred_element_type=F32)
    o_ref[...] = (g * jax.nn.sigmoid(g) * u).astype(o_ref.dtype)


def swiglu_in(xn, w_in, layer, tm=1024, tn=256):
    T, K = xn.shape
    F = w_in.shape[2] // 2
    nf = F // tn
    return pl.pallas_call(
        _swiglu_in_kernel,
        grid=(T // tm, nf),
        in_specs=[pl.BlockSpec((tm, K), lambda i, j: (i, 0)),
                  pl.BlockSpec((None, K, tn), lambda i, j: (layer, 0, j)),
                  pl.BlockSpec((None, K, tn), lambda i, j: (layer, 0, j + nf))],
        out_specs=pl.BlockSpec((tm, tn), lambda i, j: (i, j)),
        out_shape=jax.ShapeDtypeStruct((T, F), BF16),
        compiler_params=_params("parallel", "arbitrary"),
        name="swiglu_in",
    )(xn, w_in, w_in)


def _matmul_resid_kernel(a_ref, w_ref, r_ref, o_ref, *, scale):
    acc = jnp.dot(a_ref[...], w_ref[...], preferred_element_type=F32)
    o_ref[...] = r_ref[...] + scale * acc


def matmul_resid(a, w, layer, resid, scale, tm, tn):
    T, K = a.shape
    N = w.shape[2]
    return pl.pallas_call(
        functools.partial(_matmul_resid_kernel, scale=scale),
        grid=(T // tm, N // tn),
        in_specs=[pl.BlockSpec((tm, K), lambda i, j: (i, 0)),
                  pl.BlockSpec((None, K, tn), lambda i, j: (layer, 0, j)),
                  pl.BlockSpec((tm, tn), lambda i, j: (i, j))],
        out_specs=pl.BlockSpec((tm, tn), lambda i, j: (i, j)),
        out_shape=jax.ShapeDtypeStruct((T, N), F32),
        compiler_params=_params("parallel", "arbitrary"),
        name="matmul_resid",
    )(a, w, resid)


def _matmul_kernel(a_ref, w_ref, o_ref):
    o_ref[...] = jnp.dot(a_ref[...], w_ref[...], preferred_element_type=F32).astype(o_ref.dtype)


def matmul(a, w, layer, out_dtype, tm=1024, tn=512):
    T, K = a.shape
    N = w.shape[2]
    return pl.pallas_call(
        _matmul_kernel,
        grid=(T // tm, N // tn),
        in_specs=[pl.BlockSpec((tm, K), lambda i, j: (i, 0)),
                  pl.BlockSpec((None, K, tn), lambda i, j: (layer, 0, j))],
        out_specs=pl.BlockSpec((tm, tn), lambda i, j: (i, j)),
        out_shape=jax.ShapeDtypeStruct((T, N), out_dtype),
        compiler_params=_params("parallel", "arbitrary"),
        name="matmul",
    )(a, w)


def _row_iota(shape):
    return lax.broadcasted_iota(jnp.int32, shape, 0)


def _log_sigmoid(x):
    return jnp.minimum(x, 0.0) - jnp.log1p(jnp.exp(-jnp.abs(x)))


def _softplus(x):
    return jnp.maximum(x, 0.0) + jnp.log1p(jnp.exp(-jnp.abs(x)))


def _forget_cumsum_kernel(x_ref, w_ref, b_ref, o_ref, carry_ref):
    @pl.when(pl.program_id(1) == 0)
    def _():
        carry_ref[...] = jnp.zeros_like(carry_ref)

    z = jnp.dot(x_ref[...], w_ref[...], preferred_element_type=F32) + b_ref[...]
    c = _log_sigmoid(z)
    ts = c.shape[0]
    rows = _row_iota(c.shape)
    shift = 1
    while shift < ts:
        c = c + jnp.where(rows >= shift, pltpu.roll(c, shift, 0), 0.0)
        shift *= 2
    c = c + carry_ref[...]
    o_ref[...] = c
    carry_ref[...] = c[ts - 1:ts, :]


def forget_cumsum(xn, w_ff, b_ff, layer, batch, ts=512):
    T, K = xn.shape
    S = T // batch
    nc = S // ts
    P = w_ff.shape[2]
    return pl.pallas_call(
        _forget_cumsum_kernel,
        grid=(batch, nc),
        in_specs=[pl.BlockSpec((ts, K), lambda b, c: (b * nc + c, 0)),
                  pl.BlockSpec((None, K, P), lambda b, c: (layer, 0, 0)),
                  pl.BlockSpec((None, 1, P), lambda b, c: (layer, 0, 0))],
        out_specs=pl.BlockSpec((ts, P), lambda b, c: (b * nc + c, 0)),
        out_shape=jax.ShapeDtypeStruct((T, P), F32),
        scratch_shapes=[pltpu.VMEM((1, P), F32)],
        compiler_params=_params("parallel", "arbitrary"),
        name="forget_cumsum",
    )(xn, w_ff, b_ff)


def _lru_kernel(xa_ref, ga_ref, cw_ref, cb_ref, wa_ref, ba_ref, wx_ref, bx_ref, lam_ref, gn_ref,
                o_ref, tail_ref, h_ref):
    @pl.when(pl.program_id(1) == 0)
    def _():
        tail_ref[...] = jnp.zeros_like(tail_ref)
        h_ref[...] = jnp.zeros_like(h_ref)

    x = xa_ref[...]
    ts, C = x.shape
    xb = jnp.concatenate([tail_ref[...], x], axis=0)
    y = cb_ref[...] + cw_ref[CONV_WIDTH - 1:CONV_WIDTH, :] * x
    for back in range(1, CONV_WIDTH):
        y = y + cw_ref[CONV_WIDTH - 1 - back:CONV_WIDTH - back, :] * pltpu.roll(xb, back, 0)[8:, :]
    tail_ref[...] = x[ts - 8:, :]

    rs, gs = [], []
    for g in range(C // LRU_BLOCK):
        yg = y[:, g * LRU_BLOCK:(g + 1) * LRU_BLOCK].astype(BF16)
        rs.append(jnp.dot(yg, wa_ref[g], preferred_element_type=F32))
        gs.append(jnp.dot(yg, wx_ref[g], preferred_element_type=F32))
    r = jax.nn.sigmoid(jnp.concatenate(rs, axis=1) + ba_ref[...])
    i = jax.nn.sigmoid(jnp.concatenate(gs, axis=1) + bx_ref[...])

    log_a = (-LRU_C) * r * _softplus(-lam_ref[...])
    a = jnp.exp(log_a)
    th = jnp.abs(jnp.tanh(log_a))
    u = jnp.sqrt(2.0 * th / (1.0 + th)) * (i * y)

    rows = _row_iota(a.shape)
    shift = 1
    while shift < ts:
        keep = rows >= shift
        a_prev = jnp.where(keep, pltpu.roll(a, shift, 0), 1.0)
        u_prev = jnp.where(keep, pltpu.roll(u, shift, 0), 0.0)
        u = a * u_prev + u
        a = a * a_prev
        shift *= 2
    h = a * h_ref[...] + u
    h_ref[...] = h[ts - 1:ts, :]

    out = h * jax.nn.gelu(ga_ref[...])
    ms = jnp.mean(out * out, axis=-1, keepdims=True)
    o_ref[...] = (out * lax.rsqrt(ms + EPS) * gn_ref[...]).astype(o_ref.dtype)


def lru_branch(proj, conv_w, conv_b, w_a, b_a, w_x, b_x, lam, gain, batch, ts=256):
    T = proj.shape[0]
    C = proj.shape[1] // 2
    S = T // batch
    nc = S // ts
    row = lambda v: v.reshape(1, C).astype(F32)
    full2 = lambda shape: pl.BlockSpec(shape, lambda b, c: (0, 0))
    full3 = lambda shape: pl.BlockSpec(shape, lambda b, c: (0, 0, 0))
    return pl.pallas_call(
        _lru_kernel,
        grid=(batch, nc),
        in_specs=[pl.BlockSpec((ts, C), lambda b, c: (b * nc + c, 0)),
                  pl.BlockSpec((ts, C), lambda b, c: (b * nc + c, 1)),
                  full2((CONV_WIDTH, C)), full2((1, C)),
                  full3(w_a.shape), full2((1, C)),
                  full3(w_x.shape), full2((1, C)),
                  full2((1, C)), full2((1, C))],
        out_specs=pl.BlockSpec((ts, C), lambda b, c: (b * nc + c, 0)),
        out_shape=jax.ShapeDtypeStruct((T, C), BF16),
        scratch_shapes=[pltpu.VMEM((8, C), F32), pltpu.VMEM((1, C), F32)],
        compiler_params=_params("parallel", "arbitrary"),
        name="lru_branch",
    )(proj, proj, conv_w.astype(F32), row(conv_b), w_a.astype(BF16), row(b_a),
      w_x.astype(BF16), row(b_x), row(lam), row(gain))


def _softmax_pv(s_chunks, v, out_dtype):
    m = s_chunks[0]
    for sc in s_chunks[1:]:
        m = jnp.maximum(m, sc)
    m = jnp.max(m, axis=-1, keepdims=True)
    ps = [jnp.exp(sc - m) for sc in s_chunks]
    l = ps[0]
    for p in ps[1:]:
        l = l + p
    l = jnp.sum(l, axis=-1, keepdims=True)
    p = jnp.concatenate(ps, axis=1) if len(ps) > 1 else ps[0]
    pv = jnp.dot(p.astype(v.dtype), v, preferred_element_type=F32)
    return (pv / l).astype(out_dtype)


def _qk(q, k):
    return lax.dot_general(q, k, (((1,), (1,)), ((), ())), preferred_element_type=F32)


def _fox_kernel(q_ref, k_ref, v_ref, cq_ref, ck_ref, tri_ref, o_ref, *, tq, scale):
    S = q_ref.shape[0]
    for i in range(S // tq):
        kend = (i + 1) * tq
        s = _qk(q_ref[i * tq:kend, :], k_ref[0:kend, :])
        cq = cq_ref[i * tq:kend, :]
        chunks = []
        for c in range(i + 1):
            sc = s[:, c * tq:(c + 1) * tq] * scale + (cq - ck_ref[:, c * tq:(c + 1) * tq])
            if c == i:
                sc = sc + tri_ref[...]
            chunks.append(sc)
        o_ref[i * tq:kend, :] = _softmax_pv(chunks, v_ref[0:kend, :], o_ref.dtype)


def fox_attention(qkv, col0, n_heads, cum_col, cum_row, batch, tq=256):
    T = qkv.shape[0]
    S = T // batch
    hd = HEAD_DIM
    tri = np.where(np.arange(tq)[:, None] >= np.arange(tq)[None, :], 0.0, NEG_BIG).astype(np.float32)
    qkv_spec = lambda off: pl.BlockSpec((S, hd), lambda b, h: (b, col0 + off + h))
    return pl.pallas_call(
        functools.partial(_fox_kernel, tq=tq, scale=hd ** -0.5),
        grid=(batch, n_heads),
        in_specs=[qkv_spec(0), qkv_spec(n_heads), qkv_spec(2 * n_heads),
                  pl.BlockSpec((None, None, S, 1), lambda b, h: (b, h, 0, 0)),
                  pl.BlockSpec((None, None, 1, S), lambda b, h: (b, h, 0, 0)),
                  pl.BlockSpec((tq, tq), lambda b, h: (0, 0))],
        out_specs=pl.BlockSpec((S, hd), lambda b, h: (b, h)),
        out_shape=jax.ShapeDtypeStruct((T, n_heads * hd), F32),
        compiler_params=_params("parallel", "parallel"),
        name="fox_attention",
    )(qkv, qkv, qkv, cum_col, cum_row, jnp.asarray(tri))


def _dilated_kernel(q_ref, k_ref, v_ref, rc_ref, ra_ref, rb_ref, lm_ref, o_ref, qs_ref, ks_ref,
                    *, tq, scale):
    S = q_ref.shape[0]

    def rope(x_ref):
        x = x_ref[...].astype(F32)
        half = ROPE_DIM // 2
        return (x * rc_ref[...] + pltpu.roll(x, HEAD_DIM - half, 1) * ra_ref[...]
                + pltpu.roll(x, half, 1) * rb_ref[...])

    qs_ref[...] = rope(q_ref).astype(qs_ref.dtype)
    ks_ref[...] = rope(k_ref).astype(ks_ref.dtype)

    for i in range(S // tq):
        kend = (i + 1) * tq
        s = _qk(qs_ref[i * tq:kend, :], ks_ref[0:kend, :])
        chunks = [s[:, c * tq:(c + 1) * tq] * scale + lm_ref[i - c] for c in range(i + 1)]
        o_ref[i * tq:kend, :] = _softmax_pv(chunks, v_ref[0:kend, :], o_ref.dtype)


def _dilated_log_multiplicity(S, tq):
    nq = S // tq
    delta = (np.arange(nq)[:, None, None] * tq + np.arange(tq)[None, :, None]
             - np.arange(tq)[None, None, :])
    mult = np.zeros(delta.shape, np.float64)
    for window, dilation in DILATED_PATTERNS:
        mult += (delta >= 0) & (delta <= window) & (delta % dilation == 0)
    with np.errstate(divide="ignore"):
        return np.where(mult > 0, np.log(mult), NEG_BIG).astype(np.float32)


def _rope_tables(S):
    inv = 1.0 / (ROPE_THETA ** (jnp.arange(0, ROPE_DIM, 2, dtype=F32) / ROPE_DIM))
    ang = jnp.arange(S, dtype=F32)[:, None] * inv[None, :]
    cos, sin = jnp.cos(ang), jnp.sin(ang)
    half = ROPE_DIM // 2
    pad = lambda t, lo, fill: jnp.concatenate(
        [jnp.full((S, lo), fill, F32), t, jnp.full((S, HEAD_DIM - lo - t.shape[1]), fill, F32)], axis=1)
    rc = jnp.concatenate([cos, cos, jnp.ones((S, HEAD_DIM - ROPE_DIM), F32)], axis=1)
    ra = pad(-sin, 0, 0.0)
    rb = pad(sin, half, 0.0)
    return rc, ra, rb


def dilated_attention(qkv, col0, n_heads, batch, tq=256):
    T = qkv.shape[0]
    S = T // batch
    hd = HEAD_DIM
    nq = S // tq
    rc, ra, rb = _rope_tables(S)
    lm = jnp.asarray(_dilated_log_multiplicity(S, tq))
    qkv_spec = lambda off: pl.BlockSpec((S, hd), lambda b, h: (b, col0 + off + h))
    tab = pl.BlockSpec((S, hd), lambda b, h: (0, 0))
    return pl.pallas_call(
        functools.partial(_dilated_kernel, tq=tq, scale=hd ** -0.5),
        grid=(batch, n_heads),
        in_specs=[qkv_spec(0), qkv_spec(n_heads), qkv_spec(2 * n_heads), tab, tab, tab,
                  pl.BlockSpec((nq, tq, tq), lambda b, h: (0, 0, 0))],
        out_specs=pl.BlockSpec((S, hd), lambda b, h: (b, h)),
        out_shape=jax.ShapeDtypeStruct((T, n_heads * hd), F32),
        scratch_shapes=[pltpu.VMEM((S, hd), BF16), pltpu.VMEM((S, hd), BF16)],
        compiler_params=_params("parallel", "parallel"),
        name="dilated_attention",
    )(qkv, qkv, qkv, rc, ra, rb, lm)


def kernel(x, ffn1_norm, ffn1_w_in, ffn1_w_out, mix_norm, mix_w_in, conv_w, conv_b, lru_w_a, lru_b_a,
           lru_w_x, lru_b_x, lru_lam, fox_b_f, out_norm_lru, out_norm_fox, out_norm_dil, mix_w_out,
           ffn2_norm, ffn2_w_in, ffn2_w_out, final_norm):
    B, S, D = x.shape
    T = B * S
    depth = ffn1_norm.shape[0]
    d_lru = conv_w.shape[2]
    n_fox = fox_b_f.shape[1]
    d_fox = n_fox * HEAD_DIM
    d_dil = out_norm_dil.shape[1]
    n_dil = d_dil // HEAD_DIM

    ffn1_w_in_b, ffn1_w_out_b = ffn1_w_in.astype(BF16), ffn1_w_out.astype(BF16)
    ffn2_w_in_b, ffn2_w_out_b = ffn2_w_in.astype(BF16), ffn2_w_out.astype(BF16)
    mix_w_out_b = mix_w_out.astype(BF16)
    ff0 = 2 * d_lru + 3 * d_fox
    w_lru = mix_w_in[:, :, :2 * d_lru].astype(BF16)
    w_att = jnp.concatenate([mix_w_in[:, :, 2 * d_lru:ff0], mix_w_in[:, :, ff0 + n_fox:]], axis=2).astype(BF16)
    w_ff = jnp.pad(mix_w_in[:, :, ff0:ff0 + n_fox], ((0, 0), (0, 0), (0, HEAD_DIM - n_fox))).astype(BF16)
    b_ff = jnp.pad(fox_b_f.astype(F32), ((0, 0), (0, HEAD_DIM - n_fox))).reshape(depth, 1, HEAD_DIM)

    h = x.reshape(T, D)
    for l in range(depth):
        act = swiglu_in(rmsnorm(h, ffn1_norm[l], BF16), ffn1_w_in_b, l)
        h = matmul_resid(act, ffn1_w_out_b, l, h, 0.5, tm=512, tn=512)

        hn = rmsnorm(h, mix_norm[l], BF16)
        proj_lru = matmul(hn, w_lru, l, F32)
        proj_att = matmul(hn, w_att, l, BF16)
        cum = forget_cumsum(hn, w_ff, b_ff, l, B)[:, :n_fox].reshape(B, S, n_fox).transpose(0, 2, 1)
        y_a = lru_branch(proj_lru, conv_w[l], conv_b[l], lru_w_a[l], lru_b_a[l], lru_w_x[l], lru_b_x[l],
                         lru_lam[l], out_norm_lru[l], B)
        y_b = fox_attention(proj_att, 0, n_fox, cum[:, :, :, None], cum[:, :, None, :], B)
        y_c = dilated_attention(proj_att, 3 * n_fox, n_dil, B)
        y = jnp.concatenate([y_a, rmsnorm(y_b, out_norm_fox[l], BF16), rmsnorm(y_c, out_norm_dil[l], BF16)], axis=1)
        h = matmul_resid(y, mix_w_out_b, l, h, 1.0, tm=1024, tn=512)

        act = swiglu_in(rmsnorm(h, ffn2_norm[l], BF16), ffn2_w_in_b, l)
        h = matmul_resid(act, ffn2_w_out_b, l, h, 0.5, tm=512, tn=512)
    return rmsnorm(h, final_norm, x.dtype).reshape(B, S, D)
```

```python
import functools

import numpy as np
import jax
import jax.numpy as jnp
from jax import lax
from jax.experimental import pallas as pl
from jax.experimental.pallas import tpu as pltpu

F32 = jnp.float32
BF16 = jnp.bfloat16

LANES = 128
HEAD_DIM = 128
LRU_BLOCK = 128
CONV_WIDTH = 4
LRU_C = 8.0
ROPE_THETA = 500000.0
ROPE_DIM = HEAD_DIM // 4
DILATED_PATTERNS = ((128, 1), (512, 4), (2048, 16))
EPS = 1e-6
NEG_BIG = -1e30
LOG2E = 1.4426950408889634

VMEM_LIMIT_BYTES = 56 * 1024 * 1024


def _params(*sem, flags=None):
    return pltpu.CompilerParams(dimension_semantics=sem, vmem_limit_bytes=VMEM_LIMIT_BYTES, flags=flags)


def _tile(n, preferred):
    t = min(preferred, n)
    while n % t:
        t //= 2
    assert t % LANES == 0 or t == n, (n, preferred)
    return t


def _lane_partial_sumsq(x):
    x2 = x * x
    acc = x2[:, 0:LANES]
    for k in range(1, x.shape[1] // LANES):
        acc = acc + x2[:, k * LANES:(k + 1) * LANES]
    return acc


def _row_scale(ssq, width):
    return lax.rsqrt(jnp.sum(ssq, axis=-1, keepdims=True) * (1.0 / width) + EPS)


def _rowstat_kernel(x_ref, hb_ref, ssq_ref):
    x = x_ref[...]
    hb_ref[...] = x.astype(hb_ref.dtype)
    ssq_ref[...] = _lane_partial_sumsq(x)


def rowstat(x, tm=512):
    T, C = x.shape
    tm = min(tm, T)
    return pl.pallas_call(
        _rowstat_kernel,
        grid=(T // tm,),
        in_specs=[pl.BlockSpec((tm, C), lambda i: (i, 0))],
        out_specs=[pl.BlockSpec((tm, C), lambda i: (i, 0)), pl.BlockSpec((tm, LANES), lambda i: (i, 0))],
        out_shape=[jax.ShapeDtypeStruct((T, C), BF16), jax.ShapeDtypeStruct((T, LANES), F32)],
        compiler_params=_params("parallel"),
        name="rowstat",
    )(x)


def _scale_norm_kernel(x_ref, ssq_ref, g_ref, o_ref):
    x = x_ref[...]
    o_ref[...] = (x * _row_scale(ssq_ref[...], x.shape[1]) * g_ref[...]).astype(o_ref.dtype)


def scale_norm(x, ssq, g, tm=512):
    T, C = x.shape
    tm = min(tm, T)
    return pl.pallas_call(
        _scale_norm_kernel,
        grid=(T // tm,),
        in_specs=[pl.BlockSpec((tm, C), lambda i: (i, 0)),
                  pl.BlockSpec((tm, LANES), lambda i: (i, 0)),
                  pl.BlockSpec((1, C), lambda i: (0, 0))],
        out_specs=pl.BlockSpec((tm, C), lambda i: (i, 0)),
        out_shape=jax.ShapeDtypeStruct((T, C), x.dtype),
        compiler_params=_params("parallel"),
        name="scale_norm",
    )(x, ssq, g.reshape(1, C).astype(F32))


def _swiglu_in_kernel(x_ref, ssq_ref, gb_ref, wg_ref, wu_ref, o_ref):
    x = x_ref[...]
    K, tn = wg_ref.shape
    rs = _row_scale(ssq_ref[...], K)
    gb = jnp.concatenate([gb_ref[...]] * (tn // LANES), axis=1)
    g = jnp.dot(x, (wg_ref[...] * gb).astype(BF16), preferred_element_type=F32) * rs
    u = jnp.dot(x, (wu_ref[...] * gb).astype(BF16), preferred_element_type=F32) * rs
    o_ref[...] = (g * jax.nn.sigmoid(g) * u).astype(o_ref.dtype)


def swiglu_in(hb, ssq, gain, w_in, layer, tm=2048, tn=256):
    T, K = hb.shape
    tm = min(tm, T)
    F = w_in.shape[2] // 2
    nf = F // tn
    gb = jnp.broadcast_to(gain.astype(F32)[:, None], (K, LANES))
    return pl.pallas_call(
        _swiglu_in_kernel,
        grid=(T // tm, nf),
        in_specs=[pl.BlockSpec((tm, K), lambda i, j: (i, 0), pipeline_mode=pl.Buffered(1)),
                  pl.BlockSpec((tm, LANES), lambda i, j: (i, 0)),
                  pl.BlockSpec((K, LANES), lambda i, j: (0, 0)),
                  pl.BlockSpec((None, K, tn), lambda i, j: (layer, 0, j)),
                  pl.BlockSpec((None, K, tn), lambda i, j: (layer, 0, j + nf))],
        out_specs=pl.BlockSpec((tm, tn), lambda i, j: (i, j)),
        out_shape=jax.ShapeDtypeStruct((T, F), BF16),
        compiler_params=_params("parallel", "arbitrary"),
        name="swiglu_in",
    )(hb, ssq, gb, w_in, w_in)


def _emit_stream(h, j, h_ref, hb_ref, ssq_ref):
    h_ref[...] = h
    hb_ref[...] = h.astype(hb_ref.dtype)
    part = _lane_partial_sumsq(h)

    @pl.when(j == 0)
    def _():
        ssq_ref[...] = part

    @pl.when(j != 0)
    def _():
        ssq_ref[...] += part


def _matmul_resid_kernel(a_ref, w_ref, r_ref, h_ref, hb_ref, ssq_ref, *, scale):
    acc = jnp.dot(a_ref[...], w_ref[...], preferred_element_type=F32)
    _emit_stream(r_ref[...] + scale * acc, pl.program_id(1), h_ref, hb_ref, ssq_ref)


def _stream_out(T, N, tm, tn):
    specs = [pl.BlockSpec((tm, tn), lambda i, j: (i, j)),
             pl.BlockSpec((tm, tn), lambda i, j: (i, j)),
             pl.BlockSpec((tm, LANES), lambda i, j: (i, 0))]
    shapes = [jax.ShapeDtypeStruct((T, N), F32), jax.ShapeDtypeStruct((T, N), BF16),
              jax.ShapeDtypeStruct((T, LANES), F32)]
    return specs, shapes


def matmul_resid(a, w, layer, resid, scale, tm=512, tn=512):
    T, K = a.shape
    tm = min(tm, T)
    N = w.shape[2]
    tn = _tile(N, tn)
    out_specs, out_shape = _stream_out(T, N, tm, tn)
    return pl.pallas_call(
        functools.partial(_matmul_resid_kernel, scale=scale),
        grid=(T // tm, N // tn),
        in_specs=[pl.BlockSpec((tm, K), lambda i, j: (i, 0)),
                  pl.BlockSpec((None, K, tn), lambda i, j: (layer, 0, j)),
                  pl.BlockSpec((tm, tn), lambda i, j: (i, j))],
        out_specs=out_specs,
        out_shape=out_shape,
        compiler_params=_params("parallel", "arbitrary"),
        name="matmul_resid",
    )(a, w, resid)


def _mix_out_kernel(ya_ref, yb_ref, yc_ref, gb_ref, gc_ref, w_ref, r_ref, h_ref, hb_ref, ssq_ref, yn_ref):
    j = pl.program_id(1)
    ca, cb = ya_ref.shape[1], yb_ref.shape[1]

    @pl.when(j == 0)
    def _():
        def norm(y_ref, g_ref):
            y = y_ref[...]
            ms = jnp.mean(y * y, axis=-1, keepdims=True)
            return (y * lax.rsqrt(ms + EPS) * g_ref[...]).astype(yn_ref.dtype)
        yn_ref[:, 0:ca] = ya_ref[...]
        yn_ref[:, ca:ca + cb] = norm(yb_ref, gb_ref)
        yn_ref[:, ca + cb:] = norm(yc_ref, gc_ref)

    acc = jnp.dot(yn_ref[...], w_ref[...], preferred_element_type=F32)
    _emit_stream(r_ref[...] + acc, j, h_ref, hb_ref, ssq_ref)


def mix_out(ya, yb, yc, gain_b, gain_c, w, layer, resid, tm=512, tn=512):
    T, ca = ya.shape
    tm = min(tm, T)
    cb, cc = yb.shape[1], yc.shape[1]
    K, N = w.shape[1], w.shape[2]
    tn = _tile(N, tn)
    out_specs, out_shape = _stream_out(T, N, tm, tn)
    rows = lambda c: pl.BlockSpec((tm, c), lambda i, j: (i, 0))
    gain = lambda c: pl.BlockSpec((1, c), lambda i, j: (0, 0))
    return pl.pallas_call(
        _mix_out_kernel,
        grid=(T // tm, N // tn),
        in_specs=[rows(ca), rows(cb), rows(cc), gain(cb), gain(cc),
                  pl.BlockSpec((None, K, tn), lambda i, j: (layer, 0, j)),
                  pl.BlockSpec((tm, tn), lambda i, j: (i, j))],
        out_specs=out_specs,
        out_shape=out_shape,
        scratch_shapes=[pltpu.VMEM((tm, K), BF16)],
        compiler_params=_params("parallel", "arbitrary"),
        name="mix_out",
    )(ya, yb, yc, gain_b.reshape(1, cb).astype(F32), gain_c.reshape(1, cc).astype(F32), w, resid)


def _scaled_matmul_kernel(a_ref, ssq_ref, w_ref, o_ref):
    rs = _row_scale(ssq_ref[...], a_ref.shape[1])
    o_ref[...] = (jnp.dot(a_ref[...], w_ref[...], preferred_element_type=F32) * rs).astype(o_ref.dtype)


def scaled_matmul(hb, ssq, w, layer, out_dtype, tm=1024, tn=512):
    T, K = hb.shape
    tm = min(tm, T)
    N = w.shape[2]
    tn = _tile(N, tn)
    return pl.pallas_call(
        _scaled_matmul_kernel,
        grid=(T // tm, N // tn),
        in_specs=[pl.BlockSpec((tm, K), lambda i, j: (i, 0)),
                  pl.BlockSpec((tm, LANES), lambda i, j: (i, 0)),
                  pl.BlockSpec((None, K, tn), lambda i, j: (layer, 0, j))],
        out_specs=pl.BlockSpec((tm, tn), lambda i, j: (i, j)),
        out_shape=jax.ShapeDtypeStruct((T, N), out_dtype),
        compiler_params=_params("parallel", "arbitrary"),
        name="scaled_matmul",
    )(hb, ssq, w)


def _row_iota(shape):
    return lax.broadcasted_iota(jnp.int32, shape, 0)


def _log_sigmoid(x):
    return jnp.minimum(x, 0.0) - jnp.log1p(jnp.exp(-jnp.abs(x)))


def _softplus(x):
    return jnp.maximum(x, 0.0) + jnp.log1p(jnp.exp(-jnp.abs(x)))


def _forget_cumsum_kernel(x_ref, ssq_ref, w_ref, b_ref, o_ref, carry_ref):
    @pl.when(pl.program_id(1) == 0)
    def _():
        carry_ref[...] = jnp.zeros_like(carry_ref)

    rs = _row_scale(ssq_ref[...], x_ref.shape[1])
    z = jnp.dot(x_ref[...], w_ref[...], preferred_element_type=F32) * rs + b_ref[...]
    c = _log_sigmoid(z)
    ts = c.shape[0]
    rows = _row_iota(c.shape)
    shift = 1
    while shift < ts:
        c = c + jnp.where(rows >= shift, pltpu.roll(c, shift, 0), 0.0)
        shift *= 2
    c = c + carry_ref[...]
    o_ref[...] = c
    carry_ref[...] = c[ts - 1:ts, :]


def forget_cumsum(hb, ssq, w_ff, b_ff, layer, batch, ts=512):
    T, K = hb.shape
    S = T // batch
    ts = min(ts, S)
    nc = S // ts
    P = w_ff.shape[2]
    return pl.pallas_call(
        _forget_cumsum_kernel,
        grid=(batch, nc),
        in_specs=[pl.BlockSpec((ts, K), lambda b, c: (b * nc + c, 0)),
                  pl.BlockSpec((ts, LANES), lambda b, c: (b * nc + c, 0)),
                  pl.BlockSpec((None, K, P), lambda b, c: (layer, 0, 0)),
                  pl.BlockSpec((None, 1, P), lambda b, c: (layer, 0, 0))],
        out_specs=pl.BlockSpec((ts, P), lambda b, c: (b * nc + c, 0)),
        out_shape=jax.ShapeDtypeStruct((T, P), F32),
        scratch_shapes=[pltpu.VMEM((1, P), F32)],
        compiler_params=_params("parallel", "arbitrary"),
        name="forget_cumsum",
    )(hb, ssq, w_ff, b_ff)


def _lru_kernel(xa_ref, ga_ref, cw_ref, cb_ref, wa_ref, ba_ref, wx_ref, bx_ref, lam_ref, gn_ref,
                o_ref, tail_ref, h_ref):
    @pl.when(pl.program_id(1) == 0)
    def _():
        tail_ref[...] = jnp.zeros_like(tail_ref)
        h_ref[...] = jnp.zeros_like(h_ref)

    x = xa_ref[...]
    ts, C = x.shape
    xb = jnp.concatenate([tail_ref[...], x], axis=0)
    y = cb_ref[...] + cw_ref[CONV_WIDTH - 1:CONV_WIDTH, :] * x
    for back in range(1, CONV_WIDTH):
        y = y + cw_ref[CONV_WIDTH - 1 - back:CONV_WIDTH - back, :] * pltpu.roll(xb, back, 0)[8:, :]
    tail_ref[...] = x[ts - 8:, :]

    rs, gs = [], []
    for g in range(C // LRU_BLOCK):
        yg = y[:, g * LRU_BLOCK:(g + 1) * LRU_BLOCK].astype(BF16)
        rs.append(jnp.dot(yg, wa_ref[g], preferred_element_type=F32))
        gs.append(jnp.dot(yg, wx_ref[g], preferred_element_type=F32))
    r = jax.nn.sigmoid(jnp.concatenate(rs, axis=1) + ba_ref[...])
    i = jax.nn.sigmoid(jnp.concatenate(gs, axis=1) + bx_ref[...])

    log_a = (-LRU_C) * r * _softplus(-lam_ref[...])
    a = jnp.exp(log_a)
    th = jnp.abs(jnp.tanh(log_a))
    u = jnp.sqrt(2.0 * th / (1.0 + th)) * (i * y)

    rows = _row_iota(a.shape)
    shift = 1
    while shift < ts:
        keep = rows >= shift
        a_prev = jnp.where(keep, pltpu.roll(a, shift, 0), 1.0)
        u_prev = jnp.where(keep, pltpu.roll(u, shift, 0), 0.0)
        u = a * u_prev + u
        a = a * a_prev
        shift *= 2
    h = a * h_ref[...] + u
    h_ref[...] = h[ts - 1:ts, :]

    out = h * jax.nn.gelu(ga_ref[...])
    ms = jnp.mean(out * out, axis=-1, keepdims=True)
    o_ref[...] = (out * lax.rsqrt(ms + EPS) * gn_ref[...]).astype(o_ref.dtype)


def lru_branch(proj, conv_w, conv_b, w_a, b_a, w_x, b_x, lam, gain, batch, ts=256):
    T = proj.shape[0]
    C = proj.shape[1] // 2
    S = T // batch
    ts = min(ts, S)
    nc = S // ts
    row = lambda v: v.reshape(1, C).astype(F32)
    full2 = lambda shape: pl.BlockSpec(shape, lambda b, c: (0, 0))
    full3 = lambda shape: pl.BlockSpec(shape, lambda b, c: (0, 0, 0))
    return pl.pallas_call(
        _lru_kernel,
        grid=(batch, nc),
        in_specs=[pl.BlockSpec((ts, C), lambda b, c: (b * nc + c, 0)),
                  pl.BlockSpec((ts, C), lambda b, c: (b * nc + c, 1)),
                  full2((CONV_WIDTH, C)), full2((1, C)),
                  full3(w_a.shape), full2((1, C)),
                  full3(w_x.shape), full2((1, C)),
                  full2((1, C)), full2((1, C))],
        out_specs=pl.BlockSpec((ts, C), lambda b, c: (b * nc + c, 0)),
        out_shape=jax.ShapeDtypeStruct((T, C), BF16),
        scratch_shapes=[pltpu.VMEM((8, C), F32), pltpu.VMEM((1, C), F32)],
        compiler_params=_params("parallel", "arbitrary"),
        name="lru_branch",
    )(proj, proj, conv_w.astype(F32), row(conv_b), w_a.astype(BF16), row(b_a),
      w_x.astype(BF16), row(b_x), row(lam), row(gain))


def _softmax2(z_chunks, row_offset, p_dtype):
    m = z_chunks[0]
    for zc in z_chunks[1:]:
        m = jnp.maximum(m, zc)
    m = jnp.max(m, axis=-1, keepdims=True)
    shift = m if row_offset is None else (m + row_offset) - row_offset
    ps = [jnp.exp2(zc - shift) for zc in z_chunks]
    l = ps[0]
    for p in ps[1:]:
        l = l + p
    l = jnp.sum(l, axis=-1, keepdims=True)
    p = jnp.concatenate(ps, axis=1) if len(ps) > 1 else ps[0]
    return p.astype(p_dtype), l


def _qk(q, k):
    return lax.dot_general(q, k, (((1,), (1,)), ((), ())), preferred_element_type=F32)


def _causal_sweep(q_ref, k_ref, v_ref, o_ref, tq, bias_chunks, row_offset):
    nq = q_ref.shape[0] // tq
    order = list(range(nq))[::-1]

    def scores(i):
        return _qk(q_ref[i * tq:(i + 1) * tq, :], k_ref[0:(i + 1) * tq, :])

    def probs(i, s):
        z = [bias_chunks(i, c, s[:, c * tq:(c + 1) * tq]) for c in range(i + 1)]
        return _softmax2(z, None if row_offset is None else row_offset(i), v_ref.dtype)

    def values(i, p, l):
        pv = jnp.dot(p, v_ref[0:(i + 1) * tq, :], preferred_element_type=F32)
        o_ref[i * tq:(i + 1) * tq, :] = (pv / l).astype(o_ref.dtype)

    s_next, pending = scores(order[0]), None
    for n, i in enumerate(order):
        s = s_next
        if n + 1 < nq:
            s_next = scores(order[n + 1])
        p, l = probs(i, s)
        if pending is not None:
            values(*pending)
        pending = (i, p, l)
    values(*pending)


def _fox_kernel(q_ref, k_ref, v_ref, cq_ref, ck_ref, tri_ref, o_ref, *, tq):
    ck2 = ck_ref[...] * LOG2E

    def bias_chunks(i, c, s):
        z = s - ck2[:, c * tq:(c + 1) * tq]
        return z + tri_ref[...] if c == i else z

    def row_offset(i):
        return cq_ref[i * tq:(i + 1) * tq, :] * LOG2E

    _causal_sweep(q_ref, k_ref, v_ref, o_ref, tq, bias_chunks, row_offset)


def fox_attention(qkv, col0, n_heads, cum_col, cum_row, batch, tq=256):
    T = qkv.shape[0]
    S = T // batch
    tq = min(tq, S)
    hd = HEAD_DIM
    tri = np.where(np.arange(tq)[:, None] >= np.arange(tq)[None, :], 0.0, NEG_BIG).astype(np.float32)
    qkv_spec = lambda off: pl.BlockSpec((S, hd), lambda b, h: (b, col0 + off + h))
    return pl.pallas_call(
        functools.partial(_fox_kernel, tq=tq),
        grid=(batch, n_heads),
        in_specs=[qkv_spec(0), qkv_spec(n_heads), qkv_spec(2 * n_heads),
                  pl.BlockSpec((None, None, S, 1), lambda b, h: (b, h, 0, 0)),
                  pl.BlockSpec((None, None, 1, S), lambda b, h: (b, h, 0, 0)),
                  pl.BlockSpec((tq, tq), lambda b, h: (0, 0))],
        out_specs=pl.BlockSpec((S, hd), lambda b, h: (b, h)),
        out_shape=jax.ShapeDtypeStruct((T, n_heads * hd), F32),
        compiler_params=_params("parallel", "parallel"),
        name="fox_attention",
    )(qkv, qkv, qkv, cum_col, cum_row, jnp.asarray(tri))


def _dilated_kernel(q_ref, k_ref, v_ref, rc_ref, ra_ref, rb_ref, lm_ref, o_ref, qs_ref, ks_ref, *, tq):
    def rope(x_ref):
        x = x_ref[...].astype(F32)
        half = ROPE_DIM // 2
        return (x * rc_ref[...] + pltpu.roll(x, HEAD_DIM - half, 1) * ra_ref[...]
                + pltpu.roll(x, half, 1) * rb_ref[...])

    qs_ref[...] = rope(q_ref).astype(qs_ref.dtype)
    ks_ref[...] = rope(k_ref).astype(ks_ref.dtype)
    _causal_sweep(qs_ref, ks_ref, v_ref, o_ref, tq, lambda i, c, s: s + lm_ref[i - c], None)


def _dilated_log2_multiplicity(S, tq):
    nq = S // tq
    delta = (np.arange(nq)[:, None, None] * tq + np.arange(tq)[None, :, None]
             - np.arange(tq)[None, None, :])
    mult = np.zeros(delta.shape, np.float64)
    for window, dilation in DILATED_PATTERNS:
        mult += (delta >= 0) & (delta <= window) & (delta % dilation == 0)
    with np.errstate(divide="ignore"):
        return np.where(mult > 0, np.log2(mult), NEG_BIG).astype(np.float32)


def _rope_tables(S):
    inv = 1.0 / (ROPE_THETA ** (jnp.arange(0, ROPE_DIM, 2, dtype=F32) / ROPE_DIM))
    ang = jnp.arange(S, dtype=F32)[:, None] * inv[None, :]
    cos, sin = jnp.cos(ang), jnp.sin(ang)
    half = ROPE_DIM // 2
    pad = lambda t, lo: jnp.concatenate(
        [jnp.zeros((S, lo), F32), t, jnp.zeros((S, HEAD_DIM - lo - t.shape[1]), F32)], axis=1)
    rc = jnp.concatenate([cos, cos, jnp.ones((S, HEAD_DIM - ROPE_DIM), F32)], axis=1)
    ra = pad(-sin, 0)
    rb = pad(sin, half)
    return rc, ra, rb


def dilated_attention(qkv, col0, n_heads, batch, tq=256):
    T = qkv.shape[0]
    S = T // batch
    tq = min(tq, S)
    hd = HEAD_DIM
    nq = S // tq
    rc, ra, rb = _rope_tables(S)
    lm = jnp.asarray(_dilated_log2_multiplicity(S, tq))
    qkv_spec = lambda off: pl.BlockSpec((S, hd), lambda b, h: (b, col0 + off + h))
    tab = pl.BlockSpec((S, hd), lambda b, h: (0, 0))
    return pl.pallas_call(
        functools.partial(_dilated_kernel, tq=tq),
        grid=(batch, n_heads),
        in_specs=[qkv_spec(0), qkv_spec(n_heads), qkv_spec(2 * n_heads), tab, tab, tab,
                  pl.BlockSpec((nq, tq, tq), lambda b, h: (0, 0, 0))],
        out_specs=pl.BlockSpec((S, hd), lambda b, h: (b, h)),
        out_shape=jax.ShapeDtypeStruct((T, n_heads * hd), F32),
        scratch_shapes=[pltpu.VMEM((S, hd), BF16), pltpu.VMEM((S, hd), BF16)],
        compiler_params=_params("parallel", "parallel"),
        name="dilated_attention",
    )(qkv, qkv, qkv, rc, ra, rb, lm)


def kernel(x, ffn1_norm, ffn1_w_in, ffn1_w_out, mix_norm, mix_w_in, conv_w, conv_b, lru_w_a, lru_b_a,
           lru_w_x, lru_b_x, lru_lam, fox_b_f, out_norm_lru, out_norm_fox, out_norm_dil, mix_w_out,
           ffn2_norm, ffn2_w_in, ffn2_w_out, final_norm):
    B, S, D = x.shape
    T = B * S
    depth = ffn1_norm.shape[0]
    d_lru = conv_w.shape[2]
    n_fox = fox_b_f.shape[1]
    d_fox = n_fox * HEAD_DIM
    d_dil = out_norm_dil.shape[1]
    n_dil = d_dil // HEAD_DIM

    ffn1_w_out_b, ffn2_w_out_b = ffn1_w_out.astype(BF16), ffn2_w_out.astype(BF16)
    mix_w_out_b = mix_w_out.astype(BF16)
    wg = mix_w_in * mix_norm.astype(F32)[:, :, None]
    ff0 = 2 * d_lru + 3 * d_fox
    qs = HEAD_DIM ** -0.5 * LOG2E
    w_lru = wg[:, :, :2 * d_lru].astype(BF16)
    w_att = jnp.concatenate([wg[:, :, 2 * d_lru:2 * d_lru + d_fox] * qs, wg[:, :, 2 * d_lru + d_fox:ff0],
                             wg[:, :, ff0 + n_fox:ff0 + n_fox + d_dil] * qs, wg[:, :, ff0 + n_fox + d_dil:]],
                            axis=2).astype(BF16)
    w_ff = jnp.pad(wg[:, :, ff0:ff0 + n_fox], ((0, 0), (0, 0), (0, LANES - n_fox))).astype(BF16)
    b_ff = jnp.pad(fox_b_f.astype(F32), ((0, 0), (0, LANES - n_fox))).reshape(depth, 1, LANES)

    h = x.reshape(T, D)
    hb, ssq = rowstat(h)
    for l in range(depth):
        act = swiglu_in(hb, ssq, ffn1_norm[l], ffn1_w_in, l)
        h, hb, ssq = matmul_resid(act, ffn1_w_out_b, l, h, 0.5)

        proj_lru = scaled_matmul(hb, ssq, w_lru, l, F32)
        proj_att = scaled_matmul(hb, ssq, w_att, l, BF16)
        cum = forget_cumsum(hb, ssq, w_ff, b_ff, l, B)[:, :n_fox].reshape(B, S, n_fox).transpose(0, 2, 1)
        y_a = lru_branch(proj_lru, conv_w[l], conv_b[l], lru_w_a[l], lru_b_a[l], lru_w_x[l], lru_b_x[l],
                         lru_lam[l], out_norm_lru[l], B)
        y_b = fox_attention(proj_att, 0, n_fox, cum[:, :, :, None], cum[:, :, None, :], B)
        y_c = dilated_attention(proj_att, 3 * n_fox, n_dil, B)
        h, hb, ssq = mix_out(y_a, y_b, y_c, out_norm_fox[l], out_norm_dil[l], mix_w_out_b, l, h)

        act = swiglu_in(hb, ssq, ffn2_norm[l], ffn2_w_in, l)
        h, hb, ssq = matmul_resid(act, ffn2_w_out_b, l, h, 0.5)
    return scale_norm(h, ssq, final_norm).reshape(B, S, D)
```

```python
import functools

import numpy as np
import jax
import jax.numpy as jnp
from jax import lax
from jax.experimental import pallas as pl
from jax.experimental.pallas import tpu as pltpu

F32 = jnp.float32
BF16 = jnp.bfloat16

LANES = 128
HEAD_DIM = 128
LRU_BLOCK = 128
CONV_WIDTH = 4
LRU_C = 8.0
ROPE_THETA = 500000.0
ROPE_DIM = HEAD_DIM // 4
DILATED_PATTERNS = ((128, 1), (512, 4), (2048, 16))
EPS = 1e-6
NEG_BIG = -1e30
LOG2E = 1.4426950408889634
SWIGLU_ROW_CHUNK = 256

VMEM_BYTES_V7X = 64 * 1024 * 1024
VMEM_LIMIT_BYTES = VMEM_BYTES_V7X - 8 * 1024 * 1024
VMEM_LIMIT_WIDE_BYTES = VMEM_BYTES_V7X - 2 * 1024 * 1024


def _params(*sem, vmem_limit=None):
    return pltpu.CompilerParams(dimension_semantics=sem, vmem_limit_bytes=vmem_limit or VMEM_LIMIT_BYTES)


def _tile(n, preferred):
    t = min(preferred, n)
    while n % t:
        t //= 2
    assert t % LANES == 0 or t == n, (n, preferred)
    return t


def _lane_partial_sumsq(x):
    x2 = x * x
    acc = x2[:, 0:LANES]
    for k in range(1, x.shape[1] // LANES):
        acc = acc + x2[:, k * LANES:(k + 1) * LANES]
    return acc


def _row_scale(ssq, width):
    return lax.rsqrt(jnp.sum(ssq, axis=-1, keepdims=True) * (1.0 / width) + EPS)


def _rowstat_kernel(x_ref, hb_ref, ssq_ref):
    x = x_ref[...]
    hb_ref[...] = x.astype(hb_ref.dtype)
    ssq_ref[...] = _lane_partial_sumsq(x)


def rowstat(x, tm=512):
    T, C = x.shape
    tm = min(tm, T)
    return pl.pallas_call(
        _rowstat_kernel,
        grid=(T // tm,),
        in_specs=[pl.BlockSpec((tm, C), lambda i: (i, 0))],
        out_specs=[pl.BlockSpec((tm, C), lambda i: (i, 0)), pl.BlockSpec((tm, LANES), lambda i: (i, 0))],
        out_shape=[jax.ShapeDtypeStruct((T, C), BF16), jax.ShapeDtypeStruct((T, LANES), F32)],
        compiler_params=_params("parallel"),
        name="rowstat",
    )(x)


def _scale_norm_kernel(x_ref, ssq_ref, g_ref, o_ref):
    x = x_ref[...]
    o_ref[...] = (x * _row_scale(ssq_ref[...], x.shape[1]) * g_ref[...]).astype(o_ref.dtype)


def scale_norm(x, ssq, g, tm=512):
    T, C = x.shape
    tm = min(tm, T)
    return pl.pallas_call(
        _scale_norm_kernel,
        grid=(T // tm,),
        in_specs=[pl.BlockSpec((tm, C), lambda i: (i, 0)),
                  pl.BlockSpec((tm, LANES), lambda i: (i, 0)),
                  pl.BlockSpec((1, C), lambda i: (0, 0))],
        out_specs=pl.BlockSpec((tm, C), lambda i: (i, 0)),
        out_shape=jax.ShapeDtypeStruct((T, C), x.dtype),
        compiler_params=_params("parallel"),
        name="scale_norm",
    )(x, ssq, g.reshape(1, C).astype(F32))


def _swiglu_in_kernel(x_ref, ssq_ref, gb_ref, wg_ref, wu_ref, wo_ref, o_ref, wob_ref):
    K, tn = wg_ref.shape
    gb = jnp.concatenate([gb_ref[...]] * (tn // LANES), axis=1)
    wg = (wg_ref[...] * gb).astype(BF16)
    wu = (wu_ref[...] * gb).astype(BF16)
    rc = min(SWIGLU_ROW_CHUNK, x_ref.shape[0])
    for r in range(x_ref.shape[0] // rc):
        rows = slice(r * rc, (r + 1) * rc)
        x = x_ref[rows, :]
        rs = _row_scale(ssq_ref[rows, :], K)
        g = jnp.dot(x, wg, preferred_element_type=F32) * rs
        u = jnp.dot(x, wu, preferred_element_type=F32) * rs
        o_ref[rows, :] = (g * jax.nn.sigmoid(g) * u).astype(o_ref.dtype)
    wob_ref[...] = wo_ref[...].astype(wob_ref.dtype)


def swiglu_in(hb, ssq, gain, w_in, w_out, layer, tm=2048, tn=256):
    T, K = hb.shape
    tm = min(tm, T)
    F = w_in.shape[2] // 2
    N = w_out.shape[2]
    nf = F // tn
    steps = (T // tm) * nf
    slab = F // steps
    assert F % steps == 0 and slab % 16 == 0, (F, steps)
    gb = jnp.broadcast_to(gain.astype(F32)[:, None], (K, LANES))
    return pl.pallas_call(
        _swiglu_in_kernel,
        grid=(T // tm, nf),
        in_specs=[pl.BlockSpec((tm, K), lambda i, j: (i, 0)),
                  pl.BlockSpec((tm, LANES), lambda i, j: (i, 0)),
                  pl.BlockSpec((K, LANES), lambda i, j: (0, 0)),
                  pl.BlockSpec((None, K, tn), lambda i, j: (layer, 0, j)),
                  pl.BlockSpec((None, K, tn), lambda i, j: (layer, 0, j + nf)),
                  pl.BlockSpec((None, slab, N), lambda i, j: (layer, i * nf + j, 0))],
        out_specs=[pl.BlockSpec((tm, tn), lambda i, j: (i, j)),
                   pl.BlockSpec((slab, N), lambda i, j: (i * nf + j, 0))],
        out_shape=[jax.ShapeDtypeStruct((T, F), BF16), jax.ShapeDtypeStruct((F, N), BF16)],
        compiler_params=_params("parallel", "arbitrary", vmem_limit=VMEM_LIMIT_WIDE_BYTES),
        name="swiglu_in",
    )(hb, ssq, gb, w_in, w_in, w_out)


def _emit_stream(h, j, h_ref, hb_ref, ssq_ref):
    h_ref[...] = h
    hb_ref[...] = h.astype(hb_ref.dtype)
    part = _lane_partial_sumsq(h)

    @pl.when(j == 0)
    def _():
        ssq_ref[...] = part

    @pl.when(j != 0)
    def _():
        ssq_ref[...] += part


def _matmul_resid_kernel(a_ref, w_ref, r_ref, h_ref, hb_ref, ssq_ref, *, scale):
    acc = jnp.dot(a_ref[...], w_ref[...], preferred_element_type=F32)
    _emit_stream(r_ref[...] + scale * acc, pl.program_id(1), h_ref, hb_ref, ssq_ref)


def _stream_out(T, N, tm, tn):
    specs = [pl.BlockSpec((tm, tn), lambda i, j: (i, j)),
             pl.BlockSpec((tm, tn), lambda i, j: (i, j)),
             pl.BlockSpec((tm, LANES), lambda i, j: (i, 0))]
    shapes = [jax.ShapeDtypeStruct((T, N), F32), jax.ShapeDtypeStruct((T, N), BF16),
              jax.ShapeDtypeStruct((T, LANES), F32)]
    return specs, shapes


def matmul_resid(a, w, resid, scale, tm=512, tn=512):
    T, K = a.shape
    tm = min(tm, T)
    N = w.shape[1]
    tn = _tile(N, tn)
    out_specs, out_shape = _stream_out(T, N, tm, tn)
    return pl.pallas_call(
        functools.partial(_matmul_resid_kernel, scale=scale),
        grid=(T // tm, N // tn),
        in_specs=[pl.BlockSpec((tm, K), lambda i, j: (i, 0)),
                  pl.BlockSpec((K, tn), lambda i, j: (0, j)),
                  pl.BlockSpec((tm, tn), lambda i, j: (i, j))],
        out_specs=out_specs,
        out_shape=out_shape,
        compiler_params=_params("parallel", "arbitrary"),
        name="matmul_resid",
    )(a, w, resid)


def _mix_out_kernel(ya_ref, yb_ref, yc_ref, gb_ref, gc_ref, w_ref, r_ref, h_ref, hb_ref, ssq_ref, yn_ref):
    j = pl.program_id(1)
    ca, cb = ya_ref.shape[1], yb_ref.shape[1]

    @pl.when(j == 0)
    def _():
        def norm(y_ref, g_ref):
            y = y_ref[...]
            ms = jnp.mean(y * y, axis=-1, keepdims=True)
            return (y * lax.rsqrt(ms + EPS) * g_ref[...]).astype(yn_ref.dtype)
        yn_ref[:, 0:ca] = ya_ref[...]
        yn_ref[:, ca:ca + cb] = norm(yb_ref, gb_ref)
        yn_ref[:, ca + cb:] = norm(yc_ref, gc_ref)

    acc = jnp.dot(yn_ref[...], w_ref[...], preferred_element_type=F32)
    _emit_stream(r_ref[...] + acc, j, h_ref, hb_ref, ssq_ref)


def mix_out(ya, yb, yc, gain_b, gain_c, w, layer, resid, tm=512, tn=1024):
    T, ca = ya.shape
    tm = min(tm, T)
    cb, cc = yb.shape[1], yc.shape[1]
    K, N = w.shape[1], w.shape[2]
    tn = _tile(N, tn)
    out_specs, out_shape = _stream_out(T, N, tm, tn)
    rows = lambda c: pl.BlockSpec((tm, c), lambda i, j: (i, 0))
    gain = lambda c: pl.BlockSpec((1, c), lambda i, j: (0, 0))
    return pl.pallas_call(
        _mix_out_kernel,
        grid=(T // tm, N // tn),
        in_specs=[rows(ca), rows(cb), rows(cc), gain(cb), gain(cc),
                  pl.BlockSpec((None, K, tn), lambda i, j: (layer, 0, j)),
                  pl.BlockSpec((tm, tn), lambda i, j: (i, j))],
        out_specs=out_specs,
        out_shape=out_shape,
        scratch_shapes=[pltpu.VMEM((tm, K), BF16)],
        compiler_params=_params("parallel", "arbitrary"),
        name="mix_out",
    )(ya, yb, yc, gain_b.reshape(1, cb).astype(F32), gain_c.reshape(1, cc).astype(F32), w, resid)


def _scaled_matmul_kernel(a_ref, ssq_ref, w_ref, o_ref):
    rs = _row_scale(ssq_ref[...], a_ref.shape[1])
    o_ref[...] = (jnp.dot(a_ref[...], w_ref[...], preferred_element_type=F32) * rs).astype(o_ref.dtype)


def scaled_matmul(hb, ssq, w, layer, col0, N, out_dtype, tm=1024, tn=512):
    T, K = hb.shape
    tm = min(tm, T)
    tn = _tile(N, tn)
    assert col0 % tn == 0, (col0, tn)
    j0 = col0 // tn
    return pl.pallas_call(
        _scaled_matmul_kernel,
        grid=(T // tm, N // tn),
        in_specs=[pl.BlockSpec((tm, K), lambda i, j: (i, 0)),
                  pl.BlockSpec((tm, LANES), lambda i, j: (i, 0)),
                  pl.BlockSpec((None, K, tn), lambda i, j: (layer, 0, j0 + j))],
        out_specs=pl.BlockSpec((tm, tn), lambda i, j: (i, j)),
        out_shape=jax.ShapeDtypeStruct((T, N), out_dtype),
        compiler_params=_params("parallel", "arbitrary"),
        name="scaled_matmul",
    )(hb, ssq, w)


def _row_iota(shape):
    return lax.broadcasted_iota(jnp.int32, shape, 0)


def _log_sigmoid(x):
    return jnp.minimum(x, 0.0) - jnp.log1p(jnp.exp(-jnp.abs(x)))


def _softplus(x):
    return jnp.maximum(x, 0.0) + jnp.log1p(jnp.exp(-jnp.abs(x)))


def _forget_cumsum_kernel(x_ref, ssq_ref, w_ref, b_ref, o_ref, carry_ref):
    @pl.when(pl.program_id(1) == 0)
    def _():
        carry_ref[...] = jnp.zeros_like(carry_ref)

    rs = _row_scale(ssq_ref[...], x_ref.shape[1])
    z = jnp.dot(x_ref[...], w_ref[...], preferred_element_type=F32) * rs + b_ref[...]
    c = _log_sigmoid(z)
    ts = c.shape[0]
    rows = _row_iota(c.shape)
    shift = 1
    while shift < ts:
        c = c + jnp.where(rows >= shift, pltpu.roll(c, shift, 0), 0.0)
        shift *= 2
    c = c + carry_ref[...]
    o_ref[...] = c
    carry_ref[...] = c[ts - 1:ts, :]


def forget_cumsum(hb, ssq, w_ff, b_ff, layer, batch, ts=512):
    T, K = hb.shape
    S = T // batch
    ts = min(ts, S)
    nc = S // ts
    P = w_ff.shape[2]
    return pl.pallas_call(
        _forget_cumsum_kernel,
        grid=(batch, nc),
        in_specs=[pl.BlockSpec((ts, K), lambda b, c: (b * nc + c, 0)),
                  pl.BlockSpec((ts, LANES), lambda b, c: (b * nc + c, 0)),
                  pl.BlockSpec((None, K, P), lambda b, c: (layer, 0, 0)),
                  pl.BlockSpec((None, 1, P), lambda b, c: (layer, 0, 0))],
        out_specs=pl.BlockSpec((ts, P), lambda b, c: (b * nc + c, 0)),
        out_shape=jax.ShapeDtypeStruct((T, P), F32),
        scratch_shapes=[pltpu.VMEM((1, P), F32)],
        compiler_params=_params("parallel", "arbitrary"),
        name="forget_cumsum",
    )(hb, ssq, w_ff, b_ff)


def _lru_kernel(xa_ref, ga_ref, cw_ref, cb_ref, wa_ref, ba_ref, wx_ref, bx_ref, lam_ref, gn_ref,
                o_ref, tail_ref, h_ref):
    @pl.when(pl.program_id(1) == 0)
    def _():
        tail_ref[...] = jnp.zeros_like(tail_ref)
        h_ref[...] = jnp.zeros_like(h_ref)

    x = xa_ref[...]
    ts, C = x.shape
    xb = jnp.concatenate([tail_ref[...], x], axis=0)
    y = cb_ref[...] + cw_ref[CONV_WIDTH - 1:CONV_WIDTH, :] * x
    for back in range(1, CONV_WIDTH):
        y = y + cw_ref[CONV_WIDTH - 1 - back:CONV_WIDTH - back, :] * pltpu.roll(xb, back, 0)[8:, :]
    tail_ref[...] = x[ts - 8:, :]

    rs, gs = [], []
    for g in range(C // LRU_BLOCK):
        yg = y[:, g * LRU_BLOCK:(g + 1) * LRU_BLOCK].astype(BF16)
        rs.append(jnp.dot(yg, wa_ref[g], preferred_element_type=F32))
        gs.append(jnp.dot(yg, wx_ref[g], preferred_element_type=F32))
    r = jax.nn.sigmoid(jnp.concatenate(rs, axis=1) + ba_ref[...])
    i = jax.nn.sigmoid(jnp.concatenate(gs, axis=1) + bx_ref[...])

    log_a = (-LRU_C) * r * _softplus(-lam_ref[...])
    a = jnp.exp(log_a)
    th = jnp.abs(jnp.tanh(log_a))
    u = jnp.sqrt(2.0 * th / (1.0 + th)) * (i * y)

    rows = _row_iota(a.shape)
    shift = 1
    while shift < ts:
        keep = rows >= shift
        a_prev = jnp.where(keep, pltpu.roll(a, shift, 0), 1.0)
        u_prev = jnp.where(keep, pltpu.roll(u, shift, 0), 0.0)
        u = a * u_prev + u
        a = a * a_prev
        shift *= 2
    h = a * h_ref[...] + u
    h_ref[...] = h[ts - 1:ts, :]

    out = h * jax.nn.gelu(ga_ref[...])
    ms = jnp.mean(out * out, axis=-1, keepdims=True)
    o_ref[...] = (out * lax.rsqrt(ms + EPS) * gn_ref[...]).astype(o_ref.dtype)


def lru_branch(proj, conv_w, conv_b, w_a, b_a, w_x, b_x, lam, gain, batch, ts=256):
    T = proj.shape[0]
    C = proj.shape[1] // 2
    S = T // batch
    ts = min(ts, S)
    nc = S // ts
    row = lambda v: v.reshape(1, C).astype(F32)
    full2 = lambda shape: pl.BlockSpec(shape, lambda b, c: (0, 0))
    full3 = lambda shape: pl.BlockSpec(shape, lambda b, c: (0, 0, 0))
    return pl.pallas_call(
        _lru_kernel,
        grid=(batch, nc),
        in_specs=[pl.BlockSpec((ts, C), lambda b, c: (b * nc + c, 0)),
                  pl.BlockSpec((ts, C), lambda b, c: (b * nc + c, 1)),
                  full2((CONV_WIDTH, C)), full2((1, C)),
                  full3(w_a.shape), full2((1, C)),
                  full3(w_x.shape), full2((1, C)),
                  full2((1, C)), full2((1, C))],
        out_specs=pl.BlockSpec((ts, C), lambda b, c: (b * nc + c, 0)),
        out_shape=jax.ShapeDtypeStruct((T, C), BF16),
        scratch_shapes=[pltpu.VMEM((8, C), F32), pltpu.VMEM((1, C), F32)],
        compiler_params=_params("parallel", "arbitrary"),
        name="lru_branch",
    )(proj, proj, conv_w.astype(F32), row(conv_b), w_a.astype(BF16), row(b_a),
      w_x.astype(BF16), row(b_x), row(lam), row(gain))


def _softmax2(z_chunks, row_offset, p_dtype):
    m = z_chunks[0]
    for zc in z_chunks[1:]:
        m = jnp.maximum(m, zc)
    m = jnp.max(m, axis=-1, keepdims=True)
    shift = m if row_offset is None else (m + row_offset) - row_offset
    ps = [jnp.exp2(zc - shift) for zc in z_chunks]
    l = ps[0]
    for p in ps[1:]:
        l = l + p
    l = jnp.sum(l, axis=-1, keepdims=True)
    p = jnp.concatenate(ps, axis=1) if len(ps) > 1 else ps[0]
    return p.astype(p_dtype), l


def _qk(q, k):
    return lax.dot_general(q, k, (((1,), (1,)), ((), ())), preferred_element_type=F32)


def _causal_sweep(q_ref, k_ref, v_ref, o_ref, tq, bias_chunks, row_offset):
    nq = q_ref.shape[0] // tq
    order = list(range(nq))[::-1]

    def scores(i):
        return _qk(q_ref[i * tq:(i + 1) * tq, :], k_ref[0:(i + 1) * tq, :])

    def probs(i, s):
        z = [bias_chunks(i, c, s[:, c * tq:(c + 1) * tq]) for c in range(i + 1)]
        return _softmax2(z, None if row_offset is None else row_offset(i), v_ref.dtype)

    def values(i, p, l):
        pv = jnp.dot(p, v_ref[0:(i + 1) * tq, :], preferred_element_type=F32)
        o_ref[i * tq:(i + 1) * tq, :] = (pv / l).astype(o_ref.dtype)

    s_next, pending = scores(order[0]), None
    for n, i in enumerate(order):
        s = s_next
        if n + 1 < nq:
            s_next = scores(order[n + 1])
        p, l = probs(i, s)
        if pending is not None:
            values(*pending)
        pending = (i, p, l)
    values(*pending)


def _fox_kernel(q_ref, k_ref, v_ref, cq_ref, ck_ref, tri_ref, o_ref, *, tq):
    ck2 = ck_ref[...] * LOG2E

    def bias_chunks(i, c, s):
        z = s - ck2[:, c * tq:(c + 1) * tq]
        return z + tri_ref[...] if c == i else z

    def row_offset(i):
        return cq_ref[i * tq:(i + 1) * tq, :] * LOG2E

    _causal_sweep(q_ref, k_ref, v_ref, o_ref, tq, bias_chunks, row_offset)


def fox_attention(qkv, col0, n_heads, cum_col, cum_row, batch, tq=256):
    T = qkv.shape[0]
    S = T // batch
    tq = min(tq, S)
    hd = HEAD_DIM
    tri = np.where(np.arange(tq)[:, None] >= np.arange(tq)[None, :], 0.0, NEG_BIG).astype(np.float32)
    qkv_spec = lambda off: pl.BlockSpec((S, hd), lambda b, h: (b, col0 + off + h))
    return pl.pallas_call(
        functools.partial(_fox_kernel, tq=tq),
        grid=(batch, n_heads),
        in_specs=[qkv_spec(0), qkv_spec(n_heads), qkv_spec(2 * n_heads),
                  pl.BlockSpec((None, None, S, 1), lambda b, h: (b, h, 0, 0)),
                  pl.BlockSpec((None, None, 1, S), lambda b, h: (b, h, 0, 0)),
                  pl.BlockSpec((tq, tq), lambda b, h: (0, 0))],
        out_specs=pl.BlockSpec((S, hd), lambda b, h: (b, h)),
        out_shape=jax.ShapeDtypeStruct((T, n_heads * hd), F32),
        compiler_params=_params("parallel", "parallel"),
        name="fox_attention",
    )(qkv, qkv, qkv, cum_col, cum_row, jnp.asarray(tri))


def _dilated_kernel(q_ref, k_ref, v_ref, rc_ref, ra_ref, rb_ref, lm_ref, o_ref, qs_ref, ks_ref, *, tq):
    def rope(x_ref):
        x = x_ref[...].astype(F32)
        half = ROPE_DIM // 2
        return (x * rc_ref[...] + pltpu.roll(x, HEAD_DIM - half, 1) * ra_ref[...]
                + pltpu.roll(x, half, 1) * rb_ref[...])

    qs_ref[...] = rope(q_ref).astype(qs_ref.dtype)
    ks_ref[...] = rope(k_ref).astype(ks_ref.dtype)
    _causal_sweep(qs_ref, ks_ref, v_ref, o_ref, tq, lambda i, c, s: s + lm_ref[i - c], None)


def _dilated_log2_multiplicity(S, tq):
    nq = S // tq
    delta = (np.arange(nq)[:, None, None] * tq + np.arange(tq)[None, :, None]
             - np.arange(tq)[None, None, :])
    mult = np.zeros(delta.shape, np.float64)
    for window, dilation in DILATED_PATTERNS:
        mult += (delta >= 0) & (delta <= window) & (delta % dilation == 0)
    with np.errstate(divide="ignore"):
        return np.where(mult > 0, np.log2(mult), NEG_BIG).astype(np.float32)


def _rope_tables(S):
    inv = 1.0 / (ROPE_THETA ** (jnp.arange(0, ROPE_DIM, 2, dtype=F32) / ROPE_DIM))
    ang = jnp.arange(S, dtype=F32)[:, None] * inv[None, :]
    cos, sin = jnp.cos(ang), jnp.sin(ang)
    half = ROPE_DIM // 2
    pad = lambda t, lo: jnp.concatenate(
        [jnp.zeros((S, lo), F32), t, jnp.zeros((S, HEAD_DIM - lo - t.shape[1]), F32)], axis=1)
    rc = jnp.concatenate([cos, cos, jnp.ones((S, HEAD_DIM - ROPE_DIM), F32)], axis=1)
    ra = pad(-sin, 0)
    rb = pad(sin, half)
    return rc, ra, rb


def dilated_attention(qkv, col0, n_heads, batch, tq=256):
    T = qkv.shape[0]
    S = T // batch
    tq = min(tq, S)
    hd = HEAD_DIM
    nq = S // tq
    rc, ra, rb = _rope_tables(S)
    lm = jnp.asarray(_dilated_log2_multiplicity(S, tq))
    qkv_spec = lambda off: pl.BlockSpec((S, hd), lambda b, h: (b, col0 + off + h))
    tab = pl.BlockSpec((S, hd), lambda b, h: (0, 0))
    return pl.pallas_call(
        functools.partial(_dilated_kernel, tq=tq),
        grid=(batch, n_heads),
        in_specs=[qkv_spec(0), qkv_spec(n_heads), qkv_spec(2 * n_heads), tab, tab, tab,
                  pl.BlockSpec((nq, tq, tq), lambda b, h: (0, 0, 0))],
        out_specs=pl.BlockSpec((S, hd), lambda b, h: (b, h)),
        out_shape=jax.ShapeDtypeStruct((T, n_heads * hd), F32),
        scratch_shapes=[pltpu.VMEM((S, hd), BF16), pltpu.VMEM((S, hd), BF16)],
        compiler_params=_params("parallel", "parallel"),
        name="dilated_attention",
    )(qkv, qkv, qkv, rc, ra, rb, lm)


def kernel(x, ffn1_norm, ffn1_w_in, ffn1_w_out, mix_norm, mix_w_in, conv_w, conv_b, lru_w_a, lru_b_a,
           lru_w_x, lru_b_x, lru_lam, fox_b_f, out_norm_lru, out_norm_fox, out_norm_dil, mix_w_out,
           ffn2_norm, ffn2_w_in, ffn2_w_out, final_norm):
    B, S, D = x.shape
    T = B * S
    depth = ffn1_norm.shape[0]
    d_lru = conv_w.shape[2]
    n_fox = fox_b_f.shape[1]
    d_fox = n_fox * HEAD_DIM
    d_dil = out_norm_dil.shape[1]
    n_dil = d_dil // HEAD_DIM

    mix_w_out_b = mix_w_out.astype(BF16)
    ff0 = 2 * d_lru + 3 * d_fox
    qs = HEAD_DIM ** -0.5 * LOG2E
    col_scale = jnp.ones((mix_w_in.shape[2],), F32)
    col_scale = col_scale.at[2 * d_lru:2 * d_lru + d_fox].set(qs).at[ff0 + n_fox:ff0 + n_fox + d_dil].set(qs)
    row_gain = mix_norm.astype(F32)[:, :, None]

    def prep(lo, hi):
        return (mix_w_in[:, :, lo:hi] * row_gain * col_scale[lo:hi]).astype(BF16)

    w_head = prep(0, ff0)
    w_dil = prep(ff0 + n_fox, mix_w_in.shape[2])
    w_ff = jnp.pad(prep(ff0, ff0 + n_fox), ((0, 0), (0, 0), (0, LANES - n_fox)))
    b_ff = jnp.pad(fox_b_f.astype(F32), ((0, 0), (0, LANES - n_fox))).reshape(depth, 1, LANES)

    h = x.reshape(T, D)
    hb, ssq = rowstat(h)
    for l in range(depth):
        act, w_out_b = swiglu_in(hb, ssq, ffn1_norm[l], ffn1_w_in, ffn1_w_out, l)
        h, hb, ssq = matmul_resid(act, w_out_b, h, 0.5)

        proj_lru = scaled_matmul(hb, ssq, w_head, l, 0, 2 * d_lru, F32)
        proj_fox = scaled_matmul(hb, ssq, w_head, l, 2 * d_lru, 3 * d_fox, BF16)
        proj_dil = scaled_matmul(hb, ssq, w_dil, l, 0, 3 * d_dil, BF16)
        cum = forget_cumsum(hb, ssq, w_ff, b_ff, l, B)[:, :n_fox].reshape(B, S, n_fox).transpose(0, 2, 1)
        y_a = lru_branch(proj_lru, conv_w[l], conv_b[l], lru_w_a[l], lru_b_a[l], lru_w_x[l], lru_b_x[l],
                         lru_lam[l], out_norm_lru[l], B)
        y_b = fox_attention(proj_fox, 0, n_fox, cum[:, :, :, None], cum[:, :, None, :], B)
        y_c = dilated_attention(proj_dil, 0, n_dil, B)
        h, hb, ssq = mix_out(y_a, y_b, y_c, out_norm_fox[l], out_norm_dil[l], mix_w_out_b, l, h)

        act, w_out_b = swiglu_in(hb, ssq, ffn2_norm[l], ffn2_w_in, ffn2_w_out, l)
        h, hb, ssq = matmul_resid(act, w_out_b, h, 0.5)
    return scale_norm(h, ssq, final_norm).reshape(B, S, D)
```

```python
import functools

import numpy as np
import jax
import jax.numpy as jnp
from jax import lax
from jax.experimental import pallas as pl
from jax.experimental.pallas import tpu as pltpu

F32 = jnp.float32
BF16 = jnp.bfloat16

LANES = 128
SUBLANES = 8
HEAD_DIM = 128
LRU_BLOCK = 128
CONV_WIDTH = 4
LRU_C = 8.0
ROPE_THETA = 500000.0
ROPE_DIM = HEAD_DIM // 4
DILATED_PATTERNS = ((128, 1), (512, 4), (2048, 16))
EPS = 1e-6
NEG_BIG = -1e30
LOG2E = 1.4426950408889634
SWIGLU_ROW_CHUNK = 256

VMEM_BYTES_V7X = 64 * 1024 * 1024
VMEM_LIMIT_BYTES = VMEM_BYTES_V7X - 8 * 1024 * 1024
VMEM_LIMIT_WIDE_BYTES = VMEM_BYTES_V7X - 2 * 1024 * 1024


def _params(*sem, vmem_limit=None):
    return pltpu.CompilerParams(dimension_semantics=sem, vmem_limit_bytes=vmem_limit or VMEM_LIMIT_BYTES)


def _tile(n, preferred):
    t = min(preferred, n)
    while n % t:
        t //= 2
    assert t % LANES == 0 or t == n, (n, preferred)
    return t


def _lane_partial_sumsq(x):
    x2 = x * x
    acc = x2[:, 0:LANES]
    for k in range(1, x.shape[1] // LANES):
        acc = acc + x2[:, k * LANES:(k + 1) * LANES]
    return acc


def _row_scale(ssq, width):
    return lax.rsqrt(jnp.sum(ssq, axis=-1, keepdims=True) * (1.0 / width) + EPS)


def _rowstat_kernel(x_ref, hb_ref, ssq_ref):
    x = x_ref[...]
    hb_ref[...] = x.astype(hb_ref.dtype)
    ssq_ref[...] = _lane_partial_sumsq(x)


def rowstat(x, tm=512):
    T, C = x.shape
    tm = min(tm, T)
    return pl.pallas_call(
        _rowstat_kernel,
        grid=(T // tm,),
        in_specs=[pl.BlockSpec((tm, C), lambda i: (i, 0))],
        out_specs=[pl.BlockSpec((tm, C), lambda i: (i, 0)), pl.BlockSpec((tm, LANES), lambda i: (i, 0))],
        out_shape=[jax.ShapeDtypeStruct((T, C), BF16), jax.ShapeDtypeStruct((T, LANES), F32)],
        compiler_params=_params("parallel"),
        name="rowstat",
    )(x)


def _scale_norm_kernel(x_ref, ssq_ref, g_ref, o_ref):
    x = x_ref[...]
    o_ref[...] = (x * _row_scale(ssq_ref[...], x.shape[1]) * g_ref[...]).astype(o_ref.dtype)


def scale_norm(x, ssq, g, tm=512):
    T, C = x.shape
    tm = min(tm, T)
    return pl.pallas_call(
        _scale_norm_kernel,
        grid=(T // tm,),
        in_specs=[pl.BlockSpec((tm, C), lambda i: (i, 0)),
                  pl.BlockSpec((tm, LANES), lambda i: (i, 0)),
                  pl.BlockSpec((1, C), lambda i: (0, 0))],
        out_specs=pl.BlockSpec((tm, C), lambda i: (i, 0)),
        out_shape=jax.ShapeDtypeStruct((T, C), x.dtype),
        compiler_params=_params("parallel"),
        name="scale_norm",
    )(x, ssq, g.reshape(1, C).astype(F32))


def _swiglu_in_kernel(x_ref, ssq_ref, gb_ref, wg_ref, wu_ref, wo_ref, o_ref, wob_ref):
    K, tn = wg_ref.shape
    gb = jnp.concatenate([gb_ref[...]] * (tn // LANES), axis=1)
    wg = (wg_ref[...] * gb).astype(BF16)
    wu = (wu_ref[...] * gb).astype(BF16)
    rc = min(SWIGLU_ROW_CHUNK, x_ref.shape[0])
    for r in range(x_ref.shape[0] // rc):
        rows = slice(r * rc, (r + 1) * rc)
        x = x_ref[rows, :]
        rs = _row_scale(ssq_ref[rows, :], K)
        g = jnp.dot(x, wg, preferred_element_type=F32) * rs
        u = jnp.dot(x, wu, preferred_element_type=F32) * rs
        o_ref[rows, :] = (g * jax.nn.sigmoid(g) * u).astype(o_ref.dtype)
    wob_ref[...] = wo_ref[...].astype(wob_ref.dtype)


def swiglu_in(hb, ssq, gain, w_in, w_out, layer, tm=2048, tn=256):
    T, K = hb.shape
    tm = min(tm, T)
    F = w_in.shape[2] // 2
    N = w_out.shape[2]
    nf = F // tn
    steps = (T // tm) * nf
    slab = F // steps
    assert F % steps == 0 and slab % 16 == 0, (F, steps)
    gb = jnp.broadcast_to(gain.astype(F32)[:, None], (K, LANES))
    return pl.pallas_call(
        _swiglu_in_kernel,
        grid=(T // tm, nf),
        in_specs=[pl.BlockSpec((tm, K), lambda i, j: (i, 0)),
                  pl.BlockSpec((tm, LANES), lambda i, j: (i, 0)),
                  pl.BlockSpec((K, LANES), lambda i, j: (0, 0)),
                  pl.BlockSpec((None, K, tn), lambda i, j: (layer, 0, j)),
                  pl.BlockSpec((None, K, tn), lambda i, j: (layer, 0, j + nf)),
                  pl.BlockSpec((None, slab, N), lambda i, j: (layer, i * nf + j, 0))],
        out_specs=[pl.BlockSpec((tm, tn), lambda i, j: (i, j)),
                   pl.BlockSpec((slab, N), lambda i, j: (i * nf + j, 0))],
        out_shape=[jax.ShapeDtypeStruct((T, F), BF16), jax.ShapeDtypeStruct((F, N), BF16)],
        compiler_params=_params("parallel", "arbitrary", vmem_limit=VMEM_LIMIT_WIDE_BYTES),
        name="swiglu_in",
    )(hb, ssq, gb, w_in, w_in, w_out)


def _emit_stream(h, j, h_ref, hb_ref, ssq_ref):
    h_ref[...] = h
    hb_ref[...] = h.astype(hb_ref.dtype)
    part = _lane_partial_sumsq(h)

    @pl.when(j == 0)
    def _():
        ssq_ref[...] = part

    @pl.when(j != 0)
    def _():
        ssq_ref[...] += part


def _stream_out(T, N, tm, tn):
    specs = [pl.BlockSpec((tm, tn), lambda i, j: (i, j)),
             pl.BlockSpec((tm, tn), lambda i, j: (i, j)),
             pl.BlockSpec((tm, LANES), lambda i, j: (i, 0))]
    shapes = [jax.ShapeDtypeStruct((T, N), F32), jax.ShapeDtypeStruct((T, N), BF16),
              jax.ShapeDtypeStruct((T, LANES), F32)]
    return specs, shapes


def _ffn_out_kernel(a_ref, w_ref, r_ref, h_ref, hb_ref, ssq_ref, *, scale):
    acc = jnp.dot(a_ref[...], w_ref[...], preferred_element_type=F32)
    _emit_stream(r_ref[...] + scale * acc, pl.program_id(1), h_ref, hb_ref, ssq_ref)


def ffn_out(a, w, resid, scale, tm=512, tn=512):
    T, K = a.shape
    tm = min(tm, T)
    N = w.shape[1]
    tn = _tile(N, tn)
    out_specs, out_shape = _stream_out(T, N, tm, tn)
    return pl.pallas_call(
        functools.partial(_ffn_out_kernel, scale=scale),
        grid=(T // tm, N // tn),
        in_specs=[pl.BlockSpec((tm, K), lambda i, j: (i, 0)),
                  pl.BlockSpec((K, tn), lambda i, j: (0, j)),
                  pl.BlockSpec((tm, tn), lambda i, j: (i, j))],
        out_specs=out_specs,
        out_shape=out_shape,
        compiler_params=_params("parallel", "arbitrary"),
        name="ffn_out",
    )(a, w, resid)


def _mix_out_kernel(ya_ref, yb_ref, yc_ref, gb_ref, gc_ref, w_ref, r_ref, h_ref, hb_ref, ssq_ref, yn_ref):
    j = pl.program_id(1)
    ca, cb = ya_ref.shape[1], yb_ref.shape[1]

    @pl.when(j == 0)
    def _():
        def norm(y_ref, g_ref):
            y = y_ref[...].astype(F32)
            ms = jnp.mean(y * y, axis=-1, keepdims=True)
            return (y * lax.rsqrt(ms + EPS) * g_ref[...]).astype(yn_ref.dtype)
        yn_ref[:, 0:ca] = ya_ref[...]
        yn_ref[:, ca:ca + cb] = norm(yb_ref, gb_ref)
        yn_ref[:, ca + cb:] = norm(yc_ref, gc_ref)

    acc = jnp.dot(yn_ref[...], w_ref[...], preferred_element_type=F32)
    _emit_stream(r_ref[...] + acc, j, h_ref, hb_ref, ssq_ref)


def mix_out(ya, yb, yc, gain_b, gain_c, w, layer, resid, tm=1024, tn=512):
    T, ca = ya.shape
    tm = min(tm, T)
    cb, cc = yb.shape[1], yc.shape[1]
    K, N = w.shape[1], w.shape[2]
    tn = _tile(N, tn)
    out_specs, out_shape = _stream_out(T, N, tm, tn)
    rows = lambda c: pl.BlockSpec((tm, c), lambda i, j: (i, 0))
    gain = lambda c: pl.BlockSpec((1, c), lambda i, j: (0, 0))
    return pl.pallas_call(
        _mix_out_kernel,
        grid=(T // tm, N // tn),
        in_specs=[rows(ca), rows(cb), rows(cc), gain(cb), gain(cc),
                  pl.BlockSpec((None, K, tn), lambda i, j: (layer, 0, j)),
                  pl.BlockSpec((tm, tn), lambda i, j: (i, j))],
        out_specs=out_specs,
        out_shape=out_shape,
        scratch_shapes=[pltpu.VMEM((tm, K), BF16)],
        compiler_params=_params("parallel", "arbitrary", vmem_limit=VMEM_LIMIT_WIDE_BYTES),
        name="mix_out",
    )(ya, yb, yc, gain_b.reshape(1, cb).astype(F32), gain_c.reshape(1, cc).astype(F32), w, resid)


def _mix_prep_kernel(w_ref, g_ref, head_ref, dil_ref, ff_ref, *, d_lru, d_fox, d_dil, n_fox, q_scale):
    g = g_ref[...]
    q0 = 2 * d_lru
    ff0 = q0 + 3 * d_fox
    c0 = ff0 + n_fox

    def put(o_ref, lo, hi, src, scale):
        o_ref[:, lo:hi] = (w_ref[:, src:src + (hi - lo)] * (g * scale)).astype(o_ref.dtype)

    put(head_ref, 0, q0, 0, 1.0)
    put(head_ref, q0, q0 + d_fox, q0, q_scale)
    put(head_ref, q0 + d_fox, ff0, q0 + d_fox, 1.0)
    put(dil_ref, 0, d_dil, c0, q_scale)
    put(dil_ref, d_dil, 3 * d_dil, c0 + d_dil, 1.0)
    lane = lax.broadcasted_iota(jnp.int32, ff_ref.shape, 1)
    ff_ref[...] = jnp.where(lane < n_fox, w_ref[:, ff0:ff0 + LANES] * g, 0.0).astype(ff_ref.dtype)


def mix_prep(w_in, gain, d_lru, d_fox, d_dil, n_fox, q_scale, tr=128):
    L, K, d_in = w_in.shape
    tr = min(tr, K)
    ff0 = 2 * d_lru + 3 * d_fox
    assert d_in == ff0 + n_fox + 3 * d_dil and d_in >= ff0 + LANES and n_fox <= LANES
    out = lambda c: pl.BlockSpec((None, tr, c), lambda l, i: (l, i, 0))
    return pl.pallas_call(
        functools.partial(_mix_prep_kernel, d_lru=d_lru, d_fox=d_fox, d_dil=d_dil, n_fox=n_fox, q_scale=q_scale),
        grid=(L, K // tr),
        in_specs=[out(d_in), out(1)],
        out_specs=[out(ff0), out(3 * d_dil), out(LANES)],
        out_shape=[jax.ShapeDtypeStruct((L, K, ff0), BF16), jax.ShapeDtypeStruct((L, K, 3 * d_dil), BF16),
                   jax.ShapeDtypeStruct((L, K, LANES), BF16)],
        compiler_params=_params("parallel", "parallel"),
        name="mix_prep",
    )(w_in, gain.astype(F32)[:, :, None])


def _scaled_matmul_kernel(a_ref, ssq_ref, w_ref, o_ref):
    rs = _row_scale(ssq_ref[...], a_ref.shape[1])
    o_ref[...] = (jnp.dot(a_ref[...], w_ref[...], preferred_element_type=F32) * rs).astype(o_ref.dtype)


def scaled_matmul(hb, ssq, w, layer, col0, N, out_dtype, tm=1024, tn=512):
    T, K = hb.shape
    tm = min(tm, T)
    tn = _tile(N, tn)
    assert col0 % tn == 0, (col0, tn)
    j0 = col0 // tn
    return pl.pallas_call(
        _scaled_matmul_kernel,
        grid=(T // tm, N // tn),
        in_specs=[pl.BlockSpec((tm, K), lambda i, j: (i, 0)),
                  pl.BlockSpec((tm, LANES), lambda i, j: (i, 0)),
                  pl.BlockSpec((None, K, tn), lambda i, j: (layer, 0, j0 + j))],
        out_specs=pl.BlockSpec((tm, tn), lambda i, j: (i, j)),
        out_shape=jax.ShapeDtypeStruct((T, N), out_dtype),
        compiler_params=_params("parallel", "arbitrary"),
        name="scaled_matmul",
    )(hb, ssq, w)


def _row_iota(shape):
    return lax.broadcasted_iota(jnp.int32, shape, 0)


def _log_sigmoid(x):
    return jnp.minimum(x, 0.0) - jnp.log1p(jnp.exp(-jnp.abs(x)))


def _softplus(x):
    return jnp.maximum(x, 0.0) + jnp.log1p(jnp.exp(-jnp.abs(x)))


def _forget_cumsum_kernel(x_ref, ssq_ref, w_ref, b_ref, o_ref, carry_ref):
    @pl.when(pl.program_id(1) == 0)
    def _():
        carry_ref[...] = jnp.zeros_like(carry_ref)

    rs = _row_scale(ssq_ref[...], x_ref.shape[1])
    z = jnp.dot(x_ref[...], w_ref[...], preferred_element_type=F32) * rs + b_ref[...]
    c = _log_sigmoid(z)
    ts = c.shape[0]
    rows = _row_iota(c.shape)
    shift = 1
    while shift < ts:
        c = c + jnp.where(rows >= shift, pltpu.roll(c, shift, 0), 0.0)
        shift *= 2
    c = c + carry_ref[...]
    o_ref[...] = c
    carry_ref[...] = c[ts - 1:ts, :]


def forget_cumsum(hb, ssq, w_ff, b_ff, layer, batch, ts=512):
    T, K = hb.shape
    S = T // batch
    ts = min(ts, S)
    nc = S // ts
    P = w_ff.shape[2]
    return pl.pallas_call(
        _forget_cumsum_kernel,
        grid=(batch, nc),
        in_specs=[pl.BlockSpec((ts, K), lambda b, c: (b * nc + c, 0)),
                  pl.BlockSpec((ts, LANES), lambda b, c: (b * nc + c, 0)),
                  pl.BlockSpec((None, K, P), lambda b, c: (layer, 0, 0)),
                  pl.BlockSpec((None, 1, P), lambda b, c: (layer, 0, 0))],
        out_specs=pl.BlockSpec((ts, P), lambda b, c: (b * nc + c, 0)),
        out_shape=jax.ShapeDtypeStruct((T, P), F32),
        scratch_shapes=[pltpu.VMEM((1, P), F32)],
        compiler_params=_params("parallel", "arbitrary"),
        name="forget_cumsum",
    )(hb, ssq, w_ff, b_ff)


def _lru_kernel(xa_ref, ga_ref, cw_ref, cb_ref, wa_ref, ba_ref, wx_ref, bx_ref, lam_ref, gn_ref,
                o_ref, tail_ref, h_ref):
    @pl.when(pl.program_id(1) == 0)
    def _():
        tail_ref[...] = jnp.zeros_like(tail_ref)
        h_ref[...] = jnp.zeros_like(h_ref)

    x = xa_ref[...]
    ts, C = x.shape
    xb = jnp.concatenate([tail_ref[...], x], axis=0)
    y = cb_ref[...] + cw_ref[CONV_WIDTH - 1:CONV_WIDTH, :] * x
    for back in range(1, CONV_WIDTH):
        y = y + cw_ref[CONV_WIDTH - 1 - back:CONV_WIDTH - back, :] * pltpu.roll(xb, back, 0)[SUBLANES:, :]
    tail_ref[...] = x[ts - SUBLANES:, :]

    rs, gs = [], []
    for g in range(C // LRU_BLOCK):
        yg = y[:, g * LRU_BLOCK:(g + 1) * LRU_BLOCK].astype(BF16)
        rs.append(jnp.dot(yg, wa_ref[g], preferred_element_type=F32))
        gs.append(jnp.dot(yg, wx_ref[g], preferred_element_type=F32))
    r = jax.nn.sigmoid(jnp.concatenate(rs, axis=1) + ba_ref[...])
    i = jax.nn.sigmoid(jnp.concatenate(gs, axis=1) + bx_ref[...])

    log_a = (-LRU_C) * r * _softplus(-lam_ref[...])
    a = jnp.exp(log_a)
    th = jnp.abs(jnp.tanh(log_a))
    u = jnp.sqrt(2.0 * th / (1.0 + th)) * (i * y)

    n_groups = ts // SUBLANES
    a = a.reshape(n_groups, SUBLANES, C)
    u = u.reshape(n_groups, SUBLANES, C)
    sub = lax.broadcasted_iota(jnp.int32, a.shape, 1)
    shift = 1
    while shift < SUBLANES:
        keep = sub >= shift
        a_prev = jnp.where(keep, pltpu.roll(a, shift, 1), 1.0)
        u_prev = jnp.where(keep, pltpu.roll(u, shift, 1), 0.0)
        u = a * u_prev + u
        a = a * a_prev
        shift *= 2
    carry = h_ref[...]
    groups = []
    for g in range(n_groups):
        hg = a[g] * carry + u[g]
        groups.append(hg)
        carry = hg[SUBLANES - 1:SUBLANES, :]
    h = jnp.concatenate(groups, axis=0)
    h_ref[...] = carry

    out = h * jax.nn.gelu(ga_ref[...])
    ms = jnp.mean(out * out, axis=-1, keepdims=True)
    o_ref[...] = (out * lax.rsqrt(ms + EPS) * gn_ref[...]).astype(o_ref.dtype)


def lru_branch(proj, conv_w, conv_b, w_a, b_a, w_x, b_x, lam, gain, batch, ts=256):
    T = proj.shape[0]
    C = proj.shape[1] // 2
    S = T // batch
    ts = min(ts, S)
    nc = S // ts
    row = lambda v: v.reshape(1, C).astype(F32)
    full2 = lambda shape: pl.BlockSpec(shape, lambda b, c: (0, 0))
    full3 = lambda shape: pl.BlockSpec(shape, lambda b, c: (0, 0, 0))
    return pl.pallas_call(
        _lru_kernel,
        grid=(batch, nc),
        in_specs=[pl.BlockSpec((ts, C), lambda b, c: (b * nc + c, 0)),
                  pl.BlockSpec((ts, C), lambda b, c: (b * nc + c, 1)),
                  full2((CONV_WIDTH, C)), full2((1, C)),
                  full3(w_a.shape), full2((1, C)),
                  full3(w_x.shape), full2((1, C)),
                  full2((1, C)), full2((1, C))],
        out_specs=pl.BlockSpec((ts, C), lambda b, c: (b * nc + c, 0)),
        out_shape=jax.ShapeDtypeStruct((T, C), BF16),
        scratch_shapes=[pltpu.VMEM((SUBLANES, C), F32), pltpu.VMEM((1, C), F32)],
        compiler_params=_params("parallel", "arbitrary"),
        name="lru_branch",
    )(proj, proj, conv_w.astype(F32), row(conv_b), w_a.astype(BF16), row(b_a),
      w_x.astype(BF16), row(b_x), row(lam), row(gain))


def _softmax2(z_chunks, row_offset, p_dtype):
    m = z_chunks[0]
    for zc in z_chunks[1:]:
        m = jnp.maximum(m, zc)
    m = jnp.max(m, axis=-1, keepdims=True)
    shift = m if row_offset is None else (m + row_offset) - row_offset
    ps = [jnp.exp2(zc - shift) for zc in z_chunks]
    l = ps[0]
    for p in ps[1:]:
        l = l + p
    l = jnp.sum(l, axis=-1, keepdims=True)
    p = jnp.concatenate(ps, axis=1) if len(ps) > 1 else ps[0]
    return p.astype(p_dtype), l


def _qk(q, k):
    return lax.dot_general(q, k, (((1,), (1,)), ((), ())), preferred_element_type=F32)


def _causal_sweep(q_ref, k_ref, v_ref, o_ref, tq, bias_chunks, row_offset):
    nq = q_ref.shape[0] // tq
    order = list(range(nq))[::-1]

    def scores(i):
        return _qk(q_ref[i * tq:(i + 1) * tq, :], k_ref[0:(i + 1) * tq, :])

    def probs(i, s):
        z = [bias_chunks(i, c, s[:, c * tq:(c + 1) * tq]) for c in range(i + 1)]
        return _softmax2(z, None if row_offset is None else row_offset(i), v_ref.dtype)

    def values(i, p, l):
        pv = jnp.dot(p, v_ref[0:(i + 1) * tq, :], preferred_element_type=F32)
        o_ref[i * tq:(i + 1) * tq, :] = (pv / l).astype(o_ref.dtype)

    s_next, pending = scores(order[0]), None
    for n, i in enumerate(order):
        s = s_next
        if n + 1 < nq:
            s_next = scores(order[n + 1])
        p, l = probs(i, s)
        if pending is not None:
            values(*pending)
        pending = (i, p, l)
    values(*pending)


def _fox_kernel(q_ref, k_ref, v_ref, cq_ref, ck_ref, tri_ref, o_ref, *, tq):
    ck2 = ck_ref[...] * LOG2E

    def bias_chunks(i, c, s):
        z = s - ck2[:, c * tq:(c + 1) * tq]
        return z + tri_ref[...] if c == i else z

    def row_offset(i):
        return cq_ref[i * tq:(i + 1) * tq, :] * LOG2E

    _causal_sweep(q_ref, k_ref, v_ref, o_ref, tq, bias_chunks, row_offset)


def fox_attention(qkv, col0, n_heads, cum_col, cum_row, batch, tq=256):
    T = qkv.shape[0]
    S = T // batch
    tq = min(tq, S)
    hd = HEAD_DIM
    tri = np.where(np.arange(tq)[:, None] >= np.arange(tq)[None, :], 0.0, NEG_BIG).astype(np.float32)
    qkv_spec = lambda off: pl.BlockSpec((S, hd), lambda b, h: (b, col0 + off + h))
    return pl.pallas_call(
        functools.partial(_fox_kernel, tq=tq),
        grid=(batch, n_heads),
        in_specs=[qkv_spec(0), qkv_spec(n_heads), qkv_spec(2 * n_heads),
                  pl.BlockSpec((None, None, S, 1), lambda b, h: (b, h, 0, 0)),
                  pl.BlockSpec((None, None, 1, S), lambda b, h: (b, h, 0, 0)),
                  pl.BlockSpec((tq, tq), lambda b, h: (0, 0))],
        out_specs=pl.BlockSpec((S, hd), lambda b, h: (b, h)),
        out_shape=jax.ShapeDtypeStruct((T, n_heads * hd), BF16),
        compiler_params=_params("parallel", "parallel"),
        name="fox_attention",
    )(qkv, qkv, qkv, cum_col, cum_row, jnp.asarray(tri))


def _dilated_kernel(q_ref, k_ref, v_ref, rc_ref, ra_ref, rb_ref, lm_ref, o_ref, qs_ref, ks_ref, *, tq):
    def rope(x_ref):
        x = x_ref[...].astype(F32)
        half = ROPE_DIM // 2
        return (x * rc_ref[...] + pltpu.roll(x, HEAD_DIM - half, 1) * ra_ref[...]
                + pltpu.roll(x, half, 1) * rb_ref[...])

    qs_ref[...] = rope(q_ref).astype(qs_ref.dtype)
    ks_ref[...] = rope(k_ref).astype(ks_ref.dtype)
    _causal_sweep(qs_ref, ks_ref, v_ref, o_ref, tq, lambda i, c, s: s + lm_ref[i - c], None)


def _dilated_log2_multiplicity(S, tq):
    nq = S // tq
    delta = (np.arange(nq)[:, None, None] * tq + np.arange(tq)[None, :, None]
             - np.arange(tq)[None, None, :])
    mult = np.zeros(delta.shape, np.float64)
    for window, dilation in DILATED_PATTERNS:
        mult += (delta >= 0) & (delta <= window) & (delta % dilation == 0)
    with np.errstate(divide="ignore"):
        return np.where(mult > 0, np.log2(mult), NEG_BIG).astype(np.float32)


def _rope_tables(S):
    inv = 1.0 / (ROPE_THETA ** (jnp.arange(0, ROPE_DIM, 2, dtype=F32) / ROPE_DIM))
    ang = jnp.arange(S, dtype=F32)[:, None] * inv[None, :]
    cos, sin = jnp.cos(ang), jnp.sin(ang)
    half = ROPE_DIM // 2
    pad = lambda t, lo: jnp.concatenate(
        [jnp.zeros((S, lo), F32), t, jnp.zeros((S, HEAD_DIM - lo - t.shape[1]), F32)], axis=1)
    rc = jnp.concatenate([cos, cos, jnp.ones((S, HEAD_DIM - ROPE_DIM), F32)], axis=1)
    ra = pad(-sin, 0)
    rb = pad(sin, half)
    return rc, ra, rb


def dilated_attention(qkv, col0, n_heads, batch, tq=256):
    T = qkv.shape[0]
    S = T // batch
    tq = min(tq, S)
    hd = HEAD_DIM
    nq = S // tq
    rc, ra, rb = _rope_tables(S)
    lm = jnp.asarray(_dilated_log2_multiplicity(S, tq))
    qkv_spec = lambda off: pl.BlockSpec((S, hd), lambda b, h: (b, col0 + off + h))
    tab = pl.BlockSpec((S, hd), lambda b, h: (0, 0))
    return pl.pallas_call(
        functools.partial(_dilated_kernel, tq=tq),
        grid=(batch, n_heads),
        in_specs=[qkv_spec(0), qkv_spec(n_heads), qkv_spec(2 * n_heads), tab, tab, tab,
                  pl.BlockSpec((nq, tq, tq), lambda b, h: (0, 0, 0))],
        out_specs=pl.BlockSpec((S, hd), lambda b, h: (b, h)),
        out_shape=jax.ShapeDtypeStruct((T, n_heads * hd), BF16),
        scratch_shapes=[pltpu.VMEM((S, hd), BF16), pltpu.VMEM((S, hd), BF16)],
        compiler_params=_params("parallel", "parallel"),
        name="dilated_attention",
    )(qkv, qkv, qkv, rc, ra, rb, lm)


def kernel(x, ffn1_norm, ffn1_w_in, ffn1_w_out, mix_norm, mix_w_in, conv_w, conv_b, lru_w_a, lru_b_a,
           lru_w_x, lru_b_x, lru_lam, fox_b_f, out_norm_lru, out_norm_fox, out_norm_dil, mix_w_out,
           ffn2_norm, ffn2_w_in, ffn2_w_out, final_norm):
    B, S, D = x.shape
    T = B * S
    depth = ffn1_norm.shape[0]
    d_lru = conv_w.shape[2]
    n_fox = fox_b_f.shape[1]
    d_fox = n_fox * HEAD_DIM
    d_dil = out_norm_dil.shape[1]
    n_dil = d_dil // HEAD_DIM

    mix_w_out_b = mix_w_out.astype(BF16)
    w_head, w_dil, w_ff = mix_prep(mix_w_in, mix_norm, d_lru, d_fox, d_dil, n_fox, HEAD_DIM ** -0.5 * LOG2E)
    b_ff = jnp.pad(fox_b_f.astype(F32), ((0, 0), (0, LANES - n_fox))).reshape(depth, 1, LANES)

    h = x.reshape(T, D)
    hb, ssq = rowstat(h)
    for l in range(depth):
        act, w_out_b = swiglu_in(hb, ssq, ffn1_norm[l], ffn1_w_in, ffn1_w_out, l)
        h, hb, ssq = ffn_out(act, w_out_b, h, 0.5)

        proj_lru = scaled_matmul(hb, ssq, w_head, l, 0, 2 * d_lru, F32)
        proj_fox = scaled_matmul(hb, ssq, w_head, l, 2 * d_lru, 3 * d_fox, BF16)
        proj_dil = scaled_matmul(hb, ssq, w_dil, l, 0, 3 * d_dil, BF16)
        cum = forget_cumsum(hb, ssq, w_ff, b_ff, l, B)[:, :n_fox].reshape(B, S, n_fox).transpose(0, 2, 1)
        y_a = lru_branch(proj_lru, conv_w[l], conv_b[l], lru_w_a[l], lru_b_a[l], lru_w_x[l], lru_b_x[l],
                         lru_lam[l], out_norm_lru[l], B)
        y_b = fox_attention(proj_fox, 0, n_fox, cum[:, :, :, None], cum[:, :, None, :], B)
        y_c = dilated_attention(proj_dil, 0, n_dil, B)
        h, hb, ssq = mix_out(y_a, y_b, y_c, out_norm_fox[l], out_norm_dil[l], mix_w_out_b, l, h)

        act, w_out_b = swiglu_in(hb, ssq, ffn2_norm[l], ffn2_w_in, ffn2_w_out, l)
        h, hb, ssq = ffn_out(act, w_out_b, h, 0.5)
    return scale_norm(h, ssq, final_norm).reshape(B, S, D)
```

```python
import functools
import math

import numpy as np
import jax
import jax.numpy as jnp
from jax import lax
from jax.experimental import pallas as pl
from jax.experimental.pallas import tpu as pltpu

F32 = jnp.float32
BF16 = jnp.bfloat16

LANES = 128
SUBLANES = 8
HEAD_DIM = 128
LRU_BLOCK = 128
CONV_WIDTH = 4
LRU_C = 8.0
ROPE_THETA = 500000.0
ROPE_DIM = HEAD_DIM // 4
DILATED_PATTERNS = ((128, 1), (512, 4), (2048, 16))
EPS = 1e-6
NEG_BIG = -1e30
LOG2E = 1.4426950408889634
SWIGLU_ROW_CHUNK = 256

VMEM_BYTES_V7X = 64 * 1024 * 1024
VMEM_LIMIT_BYTES = VMEM_BYTES_V7X - 8 * 1024 * 1024
VMEM_LIMIT_WIDE_BYTES = VMEM_BYTES_V7X - 2 * 1024 * 1024


def _params(*sem, vmem_limit=None):
    return pltpu.CompilerParams(dimension_semantics=sem, vmem_limit_bytes=vmem_limit or VMEM_LIMIT_BYTES)


def _tile(n, preferred):
    t = min(preferred, n)
    while n % t:
        t //= 2
    assert t % LANES == 0 or t == n, (n, preferred)
    return t


def _lane_partial_sumsq(x):
    x2 = x * x
    acc = x2[:, 0:LANES]
    for k in range(1, x.shape[1] // LANES):
        acc = acc + x2[:, k * LANES:(k + 1) * LANES]
    return acc


def _dot_nt(a, b):
    return lax.dot_general(a, b, (((1,), (1,)), ((), ())), preferred_element_type=F32)


def _row_scale(ssq, width):
    return lax.rsqrt(jnp.sum(ssq, axis=-1, keepdims=True) * (1.0 / width) + EPS)


def _rowstat_kernel(x_ref, hb_ref, ssq_ref):
    x = x_ref[...]
    hb_ref[...] = x.astype(hb_ref.dtype)
    ssq_ref[...] = _lane_partial_sumsq(x)


def rowstat(x, tm=512):
    T, C = x.shape
    tm = min(tm, T)
    return pl.pallas_call(
        _rowstat_kernel,
        grid=(T // tm,),
        in_specs=[pl.BlockSpec((tm, C), lambda i: (i, 0))],
        out_specs=[pl.BlockSpec((tm, C), lambda i: (i, 0)), pl.BlockSpec((tm, LANES), lambda i: (i, 0))],
        out_shape=[jax.ShapeDtypeStruct((T, C), BF16), jax.ShapeDtypeStruct((T, LANES), F32)],
        compiler_params=_params("parallel"),
        name="rowstat",
    )(x)


def _scale_norm_kernel(x_ref, ssq_ref, g_ref, o_ref):
    x = x_ref[...]
    o_ref[...] = (x * _row_scale(ssq_ref[...], x.shape[1]) * g_ref[...]).astype(o_ref.dtype)


def scale_norm(x, ssq, g, tm=512):
    T, C = x.shape
    tm = min(tm, T)
    return pl.pallas_call(
        _scale_norm_kernel,
        grid=(T // tm,),
        in_specs=[pl.BlockSpec((tm, C), lambda i: (i, 0)),
                  pl.BlockSpec((tm, LANES), lambda i: (i, 0)),
                  pl.BlockSpec((1, C), lambda i: (0, 0))],
        out_specs=pl.BlockSpec((tm, C), lambda i: (i, 0)),
        out_shape=jax.ShapeDtypeStruct((T, C), x.dtype),
        compiler_params=_params("parallel"),
        name="scale_norm",
    )(x, ssq, g.reshape(1, C).astype(F32))


def _swiglu_in_kernel(x_ref, ssq_ref, gb_ref, wg_ref, wu_ref, wo_ref, o_ref, wob_ref):
    K, tn = wg_ref.shape
    gb = jnp.concatenate([gb_ref[...]] * (tn // LANES), axis=1)
    wg = (wg_ref[...] * gb).astype(BF16)
    wu = (wu_ref[...] * gb).astype(BF16)
    rc = min(SWIGLU_ROW_CHUNK, x_ref.shape[0])
    for r in range(x_ref.shape[0] // rc):
        rows = slice(r * rc, (r + 1) * rc)
        x = x_ref[rows, :]
        rs = _row_scale(ssq_ref[rows, :], K)
        g = jnp.dot(x, wg, preferred_element_type=F32) * rs
        u = jnp.dot(x, wu, preferred_element_type=F32) * rs
        o_ref[rows, :] = (g * jax.nn.sigmoid(g) * u).astype(o_ref.dtype)
    wob_ref[...] = wo_ref[...].astype(wob_ref.dtype)


def swiglu_in(hb, ssq, gain, w_in, w_out, layer, tm=2048, tn=256):
    T, K = hb.shape
    tm = min(tm, T)
    F = w_in.shape[2] // 2
    N = w_out.shape[2]
    nf = F // tn
    steps = (T // tm) * nf
    slab = F // steps
    assert F % steps == 0 and slab % 16 == 0, (F, steps)
    gb = jnp.broadcast_to(gain.astype(F32)[:, None], (K, LANES))
    return pl.pallas_call(
        _swiglu_in_kernel,
        grid=(T // tm, nf),
        in_specs=[pl.BlockSpec((tm, K), lambda i, j: (i, 0)),
                  pl.BlockSpec((tm, LANES), lambda i, j: (i, 0)),
                  pl.BlockSpec((K, LANES), lambda i, j: (0, 0)),
                  pl.BlockSpec((None, K, tn), lambda i, j: (layer, 0, j)),
                  pl.BlockSpec((None, K, tn), lambda i, j: (layer, 0, j + nf)),
                  pl.BlockSpec((None, slab, N), lambda i, j: (layer, i * nf + j, 0))],
        out_specs=[pl.BlockSpec((tm, tn), lambda i, j: (i, j)),
                   pl.BlockSpec((slab, N), lambda i, j: (i * nf + j, 0))],
        out_shape=[jax.ShapeDtypeStruct((T, F), BF16), jax.ShapeDtypeStruct((F, N), BF16)],
        compiler_params=_params("parallel", "arbitrary", vmem_limit=VMEM_LIMIT_WIDE_BYTES),
        name="swiglu_in",
    )(hb, ssq, gb, w_in, w_in, w_out)


def _emit_stream(h, j, h_ref, hb_ref, ssq_ref):
    h_ref[...] = h
    hb_ref[...] = h.astype(hb_ref.dtype)
    part = _lane_partial_sumsq(h)

    @pl.when(j == 0)
    def _():
        ssq_ref[...] = part

    @pl.when(j != 0)
    def _():
        ssq_ref[...] += part


def _stream_out(T, N, tm, tn):
    specs = [pl.BlockSpec((tm, tn), lambda i, j: (i, j)),
             pl.BlockSpec((tm, tn), lambda i, j: (i, j)),
             pl.BlockSpec((tm, LANES), lambda i, j: (i, 0))]
    shapes = [jax.ShapeDtypeStruct((T, N), F32), jax.ShapeDtypeStruct((T, N), BF16),
              jax.ShapeDtypeStruct((T, LANES), F32)]
    return specs, shapes


def _ffn_out_kernel(a_ref, w_ref, r_ref, h_ref, hb_ref, ssq_ref, *, scale):
    acc = jnp.dot(a_ref[...], w_ref[...], preferred_element_type=F32)
    _emit_stream(r_ref[...] + scale * acc, pl.program_id(1), h_ref, hb_ref, ssq_ref)


def ffn_out(a, w, resid, scale, tm=1024, tn=512):
    T, K = a.shape
    tm = min(tm, T)
    N = w.shape[1]
    tn = _tile(N, tn)
    out_specs, out_shape = _stream_out(T, N, tm, tn)
    return pl.pallas_call(
        functools.partial(_ffn_out_kernel, scale=scale),
        grid=(T // tm, N // tn),
        in_specs=[pl.BlockSpec((tm, K), lambda i, j: (i, 0), pipeline_mode=pl.Buffered(1)),
                  pl.BlockSpec((K, tn), lambda i, j: (0, j)),
                  pl.BlockSpec((tm, tn), lambda i, j: (i, j))],
        out_specs=out_specs,
        out_shape=out_shape,
        compiler_params=_params("parallel", "arbitrary", vmem_limit=VMEM_LIMIT_WIDE_BYTES),
        name="ffn_out",
    )(a, w, resid)


def _mix_out_kernel(ya_ref, yb_ref, yc_ref, gb_ref, gc_ref, w_ref, r_ref, h_ref, hb_ref, ssq_ref, yn_ref):
    j = pl.program_id(1)
    ca, cb = ya_ref.shape[1], yb_ref.shape[1]

    @pl.when(j == 0)
    def _():
        def norm(y_ref, g_ref):
            y = y_ref[...].astype(F32)
            ms = jnp.mean(y * y, axis=-1, keepdims=True)
            return (y * lax.rsqrt(ms + EPS) * g_ref[...]).astype(yn_ref.dtype)
        yn_ref[:, 0:ca] = ya_ref[...]
        yn_ref[:, ca:ca + cb] = norm(yb_ref, gb_ref)
        yn_ref[:, ca + cb:] = norm(yc_ref, gc_ref)

    acc = jnp.dot(yn_ref[...], w_ref[...], preferred_element_type=F32)
    _emit_stream(r_ref[...] + acc, j, h_ref, hb_ref, ssq_ref)


def mix_out(ya, yb, yc, gain_b, gain_c, w, layer, resid, tm=1024, tn=512):
    T, ca = ya.shape
    tm = min(tm, T)
    cb, cc = yb.shape[1], yc.shape[1]
    K, N = w.shape[1], w.shape[2]
    tn = _tile(N, tn)
    out_specs, out_shape = _stream_out(T, N, tm, tn)
    rows = lambda c: pl.BlockSpec((tm, c), lambda i, j: (i, 0))
    gain = lambda c: pl.BlockSpec((1, c), lambda i, j: (0, 0))
    return pl.pallas_call(
        _mix_out_kernel,
        grid=(T // tm, N // tn),
        in_specs=[rows(ca), rows(cb), rows(cc), gain(cb), gain(cc),
                  pl.BlockSpec((None, K, tn), lambda i, j: (layer, 0, j)),
                  pl.BlockSpec((tm, tn), lambda i, j: (i, j))],
        out_specs=out_specs,
        out_shape=out_shape,
        scratch_shapes=[pltpu.VMEM((tm, K), BF16)],
        compiler_params=_params("parallel", "arbitrary", vmem_limit=VMEM_LIMIT_WIDE_BYTES),
        name="mix_out",
    )(ya, yb, yc, gain_b.reshape(1, cb).astype(F32), gain_c.reshape(1, cc).astype(F32), w, resid)


def _mix_prep_kernel(wt_ref, g_ref, o_ref, *, q_lo, q_hi, q_scale, transpose):
    i = pl.program_id(1)
    scale = jnp.where((i >= q_lo) & (i < q_hi), q_scale, 1.0).astype(F32)
    w = wt_ref[...] * (g_ref[...] * scale)
    o_ref[...] = (w.T if transpose else w).astype(o_ref.dtype)


def mix_prep(w_t, gain, row0, n_rows, q_rows, q_scale, transpose=True, tr=256):
    L, d_in, K = w_t.shape
    tr = _tile(n_rows, tr)
    assert d_in % SUBLANES == 0 and row0 % SUBLANES == 0 and q_rows[0] % tr == 0 and q_rows[1] % tr == 0
    assert row0 + n_rows <= d_in
    rows = lambda l, i: (pl.multiple_of(l * d_in + row0 + i * tr, SUBLANES), 0)
    if transpose:
        out_spec, out_dims = pl.BlockSpec((None, K, tr), lambda l, i: (l, 0, i)), (L, K, n_rows)
    else:
        out_spec, out_dims = pl.BlockSpec((None, tr, K), lambda l, i: (l, i, 0)), (L, n_rows, K)
    return pl.pallas_call(
        functools.partial(_mix_prep_kernel, q_lo=q_rows[0] // tr, q_hi=q_rows[1] // tr, q_scale=q_scale,
                          transpose=transpose),
        grid=(L, n_rows // tr),
        in_specs=[pl.BlockSpec((pl.Element(tr), pl.Element(K)), rows),
                  pl.BlockSpec((None, 1, K), lambda l, i: (l, 0, 0))],
        out_specs=out_spec,
        out_shape=jax.ShapeDtypeStruct(out_dims, BF16),
        compiler_params=_params("parallel", "parallel"),
        name="mix_prep",
    )(w_t.reshape(L * d_in, K), gain.astype(F32)[:, None, :])


def _scaled_matmul_kernel(a_ref, ssq_ref, w_ref, o_ref):
    rs = _row_scale(ssq_ref[...], a_ref.shape[1])
    o_ref[...] = (jnp.dot(a_ref[...], w_ref[...], preferred_element_type=F32) * rs).astype(o_ref.dtype)


def scaled_matmul(hb, ssq, w, layer, col0, N, out_dtype, tm=1024, tn=1024):
    T, K = hb.shape
    tm = min(tm, T)
    tn = _tile(math.gcd(N, col0), tn)
    j0 = col0 // tn
    return pl.pallas_call(
        _scaled_matmul_kernel,
        grid=(T // tm, N // tn),
        in_specs=[pl.BlockSpec((tm, K), lambda i, j: (i, 0)),
                  pl.BlockSpec((tm, LANES), lambda i, j: (i, 0)),
                  pl.BlockSpec((None, K, tn), lambda i, j: (layer, 0, j0 + j))],
        out_specs=pl.BlockSpec((tm, tn), lambda i, j: (i, j)),
        out_shape=jax.ShapeDtypeStruct((T, N), out_dtype),
        compiler_params=_params("parallel", "arbitrary"),
        name="scaled_matmul",
    )(hb, ssq, w)


def _row_iota(shape):
    return lax.broadcasted_iota(jnp.int32, shape, 0)


def _log_sigmoid(x):
    return jnp.minimum(x, 0.0) - jnp.log1p(jnp.exp(-jnp.abs(x)))


def _softplus(x):
    return jnp.maximum(x, 0.0) + jnp.log1p(jnp.exp(-jnp.abs(x)))


def _forget_cumsum_kernel(x_ref, ssq_ref, w_ref, b_ref, o_ref, carry_ref):
    @pl.when(pl.program_id(1) == 0)
    def _():
        carry_ref[...] = jnp.zeros_like(carry_ref)

    rs = _row_scale(ssq_ref[...], x_ref.shape[1])
    z = _dot_nt(x_ref[...], w_ref[...]) * rs + b_ref[...]
    c = _log_sigmoid(z)
    ts = c.shape[0]
    rows = _row_iota(c.shape)
    shift = 1
    while shift < ts:
        c = c + jnp.where(rows >= shift, pltpu.roll(c, shift, 0), 0.0)
        shift *= 2
    c = c + carry_ref[...]
    o_ref[...] = c
    carry_ref[...] = c[ts - 1:ts, :]


def forget_cumsum(hb, ssq, w_ff, b_ff, layer, batch, ts=512):
    T, K = hb.shape
    S = T // batch
    ts = min(ts, S)
    nc = S // ts
    P = w_ff.shape[1]
    return pl.pallas_call(
        _forget_cumsum_kernel,
        grid=(batch, nc),
        in_specs=[pl.BlockSpec((ts, K), lambda b, c: (b * nc + c, 0)),
                  pl.BlockSpec((ts, LANES), lambda b, c: (b * nc + c, 0)),
                  pl.BlockSpec((None, P, K), lambda b, c: (layer, 0, 0)),
                  pl.BlockSpec((None, 1, P), lambda b, c: (layer, 0, 0))],
        out_specs=pl.BlockSpec((ts, P), lambda b, c: (b * nc + c, 0)),
        out_shape=jax.ShapeDtypeStruct((T, P), F32),
        scratch_shapes=[pltpu.VMEM((1, P), F32)],
        compiler_params=_params("parallel", "arbitrary"),
        name="forget_cumsum",
    )(hb, ssq, w_ff, b_ff)


def _lru_kernel(xa_ref, ga_ref, cw_ref, cb_ref, wa_ref, ba_ref, wx_ref, bx_ref, lam_ref, gn_ref,
                o_ref, tail_ref, h_ref):
    @pl.when(pl.program_id(1) == 0)
    def _():
        tail_ref[...] = jnp.zeros_like(tail_ref)
        h_ref[...] = jnp.zeros_like(h_ref)

    x = xa_ref[...]
    ts, C = x.shape
    xb = jnp.concatenate([tail_ref[...], x], axis=0)
    y = cb_ref[...] + cw_ref[CONV_WIDTH - 1:CONV_WIDTH, :] * x
    for back in range(1, CONV_WIDTH):
        y = y + cw_ref[CONV_WIDTH - 1 - back:CONV_WIDTH - back, :] * pltpu.roll(xb, back, 0)[SUBLANES:, :]
    tail_ref[...] = x[ts - SUBLANES:, :]

    rs, gs = [], []
    for g in range(C // LRU_BLOCK):
        yg = y[:, g * LRU_BLOCK:(g + 1) * LRU_BLOCK].astype(BF16)
        rs.append(jnp.dot(yg, wa_ref[g], preferred_element_type=F32))
        gs.append(jnp.dot(yg, wx_ref[g], preferred_element_type=F32))
    r = jax.nn.sigmoid(jnp.concatenate(rs, axis=1) + ba_ref[...])
    i = jax.nn.sigmoid(jnp.concatenate(gs, axis=1) + bx_ref[...])

    log_a = (-LRU_C) * r * _softplus(-lam_ref[...])
    a = jnp.exp(log_a)
    th = jnp.abs(jnp.tanh(log_a))
    u = jnp.sqrt(2.0 * th / (1.0 + th)) * (i * y)

    n_groups = ts // SUBLANES
    a = a.reshape(n_groups, SUBLANES, C)
    u = u.reshape(n_groups, SUBLANES, C)
    sub = lax.broadcasted_iota(jnp.int32, a.shape, 1)
    shift = 1
    while shift < SUBLANES:
        keep = sub >= shift
        a_prev = jnp.where(keep, pltpu.roll(a, shift, 1), 1.0)
        u_prev = jnp.where(keep, pltpu.roll(u, shift, 1), 0.0)
        u = a * u_prev + u
        a = a * a_prev
        shift *= 2
    carry = h_ref[...]
    groups = []
    for g in range(n_groups):
        hg = a[g] * carry + u[g]
        groups.append(hg)
        carry = hg[SUBLANES - 1:SUBLANES, :]
    h = jnp.concatenate(groups, axis=0)
    h_ref[...] = carry

    out = h * jax.nn.gelu(ga_ref[...])
    ms = jnp.mean(out * out, axis=-1, keepdims=True)
    o_ref[...] = (out * lax.rsqrt(ms + EPS) * gn_ref[...]).astype(o_ref.dtype)


def lru_branch(proj, conv_w, conv_b, w_a, b_a, w_x, b_x, lam, gain, batch, ts=256):
    T = proj.shape[0]
    C = proj.shape[1] // 2
    S = T // batch
    ts = min(ts, S)
    nc = S // ts
    row = lambda v: v.reshape(1, C).astype(F32)
    full2 = lambda shape: pl.BlockSpec(shape, lambda b, c: (0, 0))
    full3 = lambda shape: pl.BlockSpec(shape, lambda b, c: (0, 0, 0))
    return pl.pallas_call(
        _lru_kernel,
        grid=(batch, nc),
        in_specs=[pl.BlockSpec((ts, C), lambda b, c: (b * nc + c, 0)),
                  pl.BlockSpec((ts, C), lambda b, c: (b * nc + c, 1)),
                  full2((CONV_WIDTH, C)), full2((1, C)),
                  full3(w_a.shape), full2((1, C)),
                  full3(w_x.shape), full2((1, C)),
                  full2((1, C)), full2((1, C))],
        out_specs=pl.BlockSpec((ts, C), lambda b, c: (b * nc + c, 0)),
        out_shape=jax.ShapeDtypeStruct((T, C), BF16),
        scratch_shapes=[pltpu.VMEM((SUBLANES, C), F32), pltpu.VMEM((1, C), F32)],
        compiler_params=_params("parallel", "arbitrary"),
        name="lru_branch",
    )(proj, proj, conv_w.astype(F32), row(conv_b), w_a.astype(BF16), row(b_a),
      w_x.astype(BF16), row(b_x), row(lam), row(gain))


def _softmax2(z_chunks, row_offset, p_dtype):
    m = z_chunks[0]
    for zc in z_chunks[1:]:
        m = jnp.maximum(m, zc)
    m = jnp.max(m, axis=-1, keepdims=True)
    shift = m if row_offset is None else (m + row_offset) - row_offset
    ps = [jnp.exp2(zc - shift) for zc in z_chunks]
    l = ps[0]
    for p in ps[1:]:
        l = l + p
    l = jnp.sum(l, axis=-1, keepdims=True)
    p = jnp.concatenate(ps, axis=1) if len(ps) > 1 else ps[0]
    return p.astype(p_dtype), l


def _causal_sweep(q_ref, k_ref, v_ref, o_ref, tq, bias_chunks, row_offset):
    nq = q_ref.shape[0] // tq
    order = list(range(nq))[::-1]

    def scores(i):
        return _dot_nt(q_ref[i * tq:(i + 1) * tq, :], k_ref[0:(i + 1) * tq, :])

    def probs(i, s):
        z = [bias_chunks(i, c, s[:, c * tq:(c + 1) * tq]) for c in range(i + 1)]
        return _softmax2(z, None if row_offset is None else row_offset(i), v_ref.dtype)

    def values(i, p, l):
        pv = jnp.dot(p, v_ref[0:(i + 1) * tq, :], preferred_element_type=F32)
        o_ref[i * tq:(i + 1) * tq, :] = (pv / l).astype(o_ref.dtype)

    s_next, pending = scores(order[0]), None
    for n, i in enumerate(order):
        s = s_next
        if n + 1 < nq:
            s_next = scores(order[n + 1])
        p, l = probs(i, s)
        if pending is not None:
            values(*pending)
        pending = (i, p, l)
    values(*pending)


def _fox_kernel(q_ref, k_ref, v_ref, cq_ref, ck_ref, tri_ref, o_ref, *, tq):
    ck2 = ck_ref[...] * LOG2E

    def bias_chunks(i, c, s):
        z = s - ck2[:, c * tq:(c + 1) * tq]
        return z + tri_ref[...] if c == i else z

    def row_offset(i):
        return cq_ref[i * tq:(i + 1) * tq, :] * LOG2E

    _causal_sweep(q_ref, k_ref, v_ref, o_ref, tq, bias_chunks, row_offset)


def fox_attention(qkv, col0, n_heads, cum_col, cum_row, batch, tq=256):
    T = qkv.shape[0]
    S = T // batch
    tq = min(tq, S)
    hd = HEAD_DIM
    tri = np.where(np.arange(tq)[:, None] >= np.arange(tq)[None, :], 0.0, NEG_BIG).astype(np.float32)
    qkv_spec = lambda off: pl.BlockSpec((S, hd), lambda b, h: (b, col0 + off + h))
    return pl.pallas_call(
        functools.partial(_fox_kernel, tq=tq),
        grid=(batch, n_heads),
        in_specs=[qkv_spec(0), qkv_spec(n_heads), qkv_spec(2 * n_heads),
                  pl.BlockSpec((None, None, S, 1), lambda b, h: (b, h, 0, 0)),
                  pl.BlockSpec((None, None, 1, S), lambda b, h: (b, h, 0, 0)),
                  pl.BlockSpec((tq, tq), lambda b, h: (0, 0))],
        out_specs=pl.BlockSpec((S, hd), lambda b, h: (b, h)),
        out_shape=jax.ShapeDtypeStruct((T, n_heads * hd), BF16),
        compiler_params=_params("parallel", "parallel"),
        name="fox_attention",
    )(qkv, qkv, qkv, cum_col, cum_row, jnp.asarray(tri))


def _dilated_kernel(q_ref, k_ref, v_ref, rc_ref, ra_ref, rb_ref, lm_ref, o_ref, qs_ref, ks_ref, *, tq):
    def rope(x_ref):
        x = x_ref[...].astype(F32)
        half = ROPE_DIM // 2
        return (x * rc_ref[...] + pltpu.roll(x, HEAD_DIM - half, 1) * ra_ref[...]
                + pltpu.roll(x, half, 1) * rb_ref[...])

    qs_ref[...] = rope(q_ref).astype(qs_ref.dtype)
    ks_ref[...] = rope(k_ref).astype(ks_ref.dtype)
    _causal_sweep(qs_ref, ks_ref, v_ref, o_ref, tq, lambda i, c, s: s + lm_ref[i - c], None)


def _dilated_log2_multiplicity(S, tq):
    nq = S // tq
    delta = (np.arange(nq)[:, None, None] * tq + np.arange(tq)[None, :, None]
             - np.arange(tq)[None, None, :])
    mult = np.zeros(delta.shape, np.float64)
    for window, dilation in DILATED_PATTERNS:
        mult += (delta >= 0) & (delta <= window) & (delta % dilation == 0)
    with np.errstate(divide="ignore"):
        return np.where(mult > 0, np.log2(mult), NEG_BIG).astype(np.float32)


def _rope_tables(S):
    inv = 1.0 / (ROPE_THETA ** (jnp.arange(0, ROPE_DIM, 2, dtype=F32) / ROPE_DIM))
    ang = jnp.arange(S, dtype=F32)[:, None] * inv[None, :]
    cos, sin = jnp.cos(ang), jnp.sin(ang)
    half = ROPE_DIM // 2
    pad = lambda t, lo: jnp.concatenate(
        [jnp.zeros((S, lo), F32), t, jnp.zeros((S, HEAD_DIM - lo - t.shape[1]), F32)], axis=1)
    rc = jnp.concatenate([cos, cos, jnp.ones((S, HEAD_DIM - ROPE_DIM), F32)], axis=1)
    ra = pad(-sin, 0)
    rb = pad(sin, half)
    return rc, ra, rb


def dilated_attention(qkv, col0, n_heads, batch, tq=256):
    T = qkv.shape[0]
    S = T // batch
    tq = min(tq, S)
    hd = HEAD_DIM
    nq = S // tq
    rc, ra, rb = _rope_tables(S)
    lm = jnp.asarray(_dilated_log2_multiplicity(S, tq))
    qkv_spec = lambda off: pl.BlockSpec((S, hd), lambda b, h: (b, col0 + off + h))
    tab = pl.BlockSpec((S, hd), lambda b, h: (0, 0))
    return pl.pallas_call(
        functools.partial(_dilated_kernel, tq=tq),
        grid=(batch, n_heads),
        in_specs=[qkv_spec(0), qkv_spec(n_heads), qkv_spec(2 * n_heads), tab, tab, tab,
                  pl.BlockSpec((nq, tq, tq), lambda b, h: (0, 0, 0))],
        out_specs=pl.BlockSpec((S, hd), lambda b, h: (b, h)),
        out_shape=jax.ShapeDtypeStruct((T, n_heads * hd), BF16),
        scratch_shapes=[pltpu.VMEM((S, hd), BF16), pltpu.VMEM((S, hd), BF16)],
        compiler_params=_params("parallel", "parallel"),
        name="dilated_attention",
    )(qkv, qkv, qkv, rc, ra, rb, lm)


def kernel(x, ffn1_norm, ffn1_w_in, ffn1_w_out, mix_norm, mix_w_in, conv_w, conv_b, lru_w_a, lru_b_a,
           lru_w_x, lru_b_x, lru_lam, fox_b_f, out_norm_lru, out_norm_fox, out_norm_dil, mix_w_out,
           ffn2_norm, ffn2_w_in, ffn2_w_out, final_norm):
    B, S, D = x.shape
    T = B * S
    depth = ffn1_norm.shape[0]
    d_lru = conv_w.shape[2]
    n_fox = fox_b_f.shape[1]
    d_fox = n_fox * HEAD_DIM
    d_dil = out_norm_dil.shape[1]
    n_dil = d_dil // HEAD_DIM

    mix_w_out_b = mix_w_out.astype(BF16)
    w_t = jnp.swapaxes(mix_w_in, 1, 2)
    q_scale = HEAD_DIM ** -0.5 * LOG2E
    ff0 = 2 * d_lru + 3 * d_fox
    w_head = mix_prep(w_t, mix_norm, 0, ff0, (2 * d_lru, 2 * d_lru + d_fox), q_scale)
    w_dil = mix_prep(w_t, mix_norm, ff0 + n_fox, 3 * d_dil, (0, d_dil), q_scale)
    w_ff = mix_prep(w_t, mix_norm, ff0, LANES, (0, 0), q_scale, transpose=False)
    b_ff = jnp.pad(fox_b_f.astype(F32), ((0, 0), (0, LANES - n_fox))).reshape(depth, 1, LANES)

    h = x.reshape(T, D)
    hb, ssq = rowstat(h)
    for l in range(depth):
        act, w_out_b = swiglu_in(hb, ssq, ffn1_norm[l], ffn1_w_in, ffn1_w_out, l)
        h, hb, ssq = ffn_out(act, w_out_b, h, 0.5)

        proj_lru = scaled_matmul(hb, ssq, w_head, l, 0, 2 * d_lru, F32)
        proj_fox = scaled_matmul(hb, ssq, w_head, l, 2 * d_lru, 3 * d_fox, BF16)
        proj_dil = scaled_matmul(hb, ssq, w_dil, l, 0, 3 * d_dil, BF16)
        cum = forget_cumsum(hb, ssq, w_ff, b_ff, l, B)[:, :n_fox].reshape(B, S, n_fox).transpose(0, 2, 1)
        y_a = lru_branch(proj_lru, conv_w[l], conv_b[l], lru_w_a[l], lru_b_a[l], lru_w_x[l], lru_b_x[l],
                         lru_lam[l], out_norm_lru[l], B)
        y_b = fox_attention(proj_fox, 0, n_fox, cum[:, :, :, None], cum[:, :, None, :], B)
        y_c = dilated_attention(proj_dil, 0, n_dil, B)
        h, hb, ssq = mix_out(y_a, y_b, y_c, out_norm_fox[l], out_norm_dil[l], mix_w_out_b, l, h)

        act, w_out_b = swiglu_in(hb, ssq, ffn2_norm[l], ffn2_w_in, ffn2_w_out, l)
        h, hb, ssq = ffn_out(act, w_out_b, h, 0.5)
    return scale_norm(h, ssq, final_norm).reshape(B, S, D)
```

```python
import functools
import math

import numpy as np
import jax
import jax.numpy as jnp
from jax import lax
from jax.experimental import pallas as pl
from jax.experimental.pallas import tpu as pltpu

F32 = jnp.float32
BF16 = jnp.bfloat16

LANES = 128
SUBLANES = 8
HEAD_DIM = 128
LRU_BLOCK = 128
CONV_WIDTH = 4
LRU_C = 8.0
ROPE_THETA = 500000.0
ROPE_DIM = HEAD_DIM // 4
DILATED_PATTERNS = ((128, 1), (512, 4), (2048, 16))
EPS = 1e-6
NEG_BIG = -1e30
LOG2E = 1.4426950408889634
SWIGLU_ROW_CHUNK = 256

VMEM_BYTES_V7X = 64 * 1024 * 1024
VMEM_LIMIT_BYTES = VMEM_BYTES_V7X - 8 * 1024 * 1024
VMEM_LIMIT_WIDE_BYTES = VMEM_BYTES_V7X - 2 * 1024 * 1024


def _params(*sem, vmem_limit=None):
    return pltpu.CompilerParams(dimension_semantics=sem, vmem_limit_bytes=vmem_limit or VMEM_LIMIT_BYTES)


def _tile(n, preferred):
    t = min(preferred, n)
    while n % t:
        t //= 2
    assert t % LANES == 0 or t == n, (n, preferred)
    return t


def _lane_partial_sumsq(x):
    x2 = x * x
    acc = x2[:, 0:LANES]
    for k in range(1, x.shape[1] // LANES):
        acc = acc + x2[:, k * LANES:(k + 1) * LANES]
    return acc


def _dot_nt(a, b):
    return lax.dot_general(a, b, (((1,), (1,)), ((), ())), preferred_element_type=F32)


def _row_scale(ssq, width):
    return lax.rsqrt(jnp.sum(ssq, axis=-1, keepdims=True) * (1.0 / width) + EPS)


def _rowstat_kernel(x_ref, hb_ref, ssq_ref):
    x = x_ref[...]
    hb_ref[...] = x.astype(hb_ref.dtype)
    ssq_ref[...] = _lane_partial_sumsq(x)


def rowstat(x, tm=512):
    T, C = x.shape
    tm = min(tm, T)
    return pl.pallas_call(
        _rowstat_kernel,
        grid=(T // tm,),
        in_specs=[pl.BlockSpec((tm, C), lambda i: (i, 0))],
        out_specs=[pl.BlockSpec((tm, C), lambda i: (i, 0)), pl.BlockSpec((tm, LANES), lambda i: (i, 0))],
        out_shape=[jax.ShapeDtypeStruct((T, C), BF16), jax.ShapeDtypeStruct((T, LANES), F32)],
        compiler_params=_params("parallel"),
        name="rowstat",
    )(x)


def _scale_norm_kernel(x_ref, ssq_ref, g_ref, o_ref):
    x = x_ref[...]
    o_ref[...] = (x * _row_scale(ssq_ref[...], x.shape[1]) * g_ref[...]).astype(o_ref.dtype)


def scale_norm(x, ssq, g, tm=512):
    T, C = x.shape
    tm = min(tm, T)
    return pl.pallas_call(
        _scale_norm_kernel,
        grid=(T // tm,),
        in_specs=[pl.BlockSpec((tm, C), lambda i: (i, 0)),
                  pl.BlockSpec((tm, LANES), lambda i: (i, 0)),
                  pl.BlockSpec((1, C), lambda i: (0, 0))],
        out_specs=pl.BlockSpec((tm, C), lambda i: (i, 0)),
        out_shape=jax.ShapeDtypeStruct((T, C), x.dtype),
        compiler_params=_params("parallel"),
        name="scale_norm",
    )(x, ssq, g.reshape(1, C).astype(F32))


def _swiglu_in_kernel(x_ref, ssq_ref, gb_ref, wg_ref, wu_ref, wo_ref, o_ref, wob_ref):
    K, tn = wg_ref.shape
    gb = jnp.concatenate([gb_ref[...]] * (tn // LANES), axis=1)
    wg = (wg_ref[...] * gb).astype(BF16)
    wu = (wu_ref[...] * gb).astype(BF16)
    rc = min(SWIGLU_ROW_CHUNK, x_ref.shape[0])
    for r in range(x_ref.shape[0] // rc):
        rows = slice(r * rc, (r + 1) * rc)
        x = x_ref[rows, :]
        rs = _row_scale(ssq_ref[rows, :], K)
        g = jnp.dot(x, wg, preferred_element_type=F32) * rs
        u = jnp.dot(x, wu, preferred_element_type=F32) * rs
        o_ref[rows, :] = (g * jax.nn.sigmoid(g) * u).astype(o_ref.dtype)
    wob_ref[...] = wo_ref[...].astype(wob_ref.dtype)


def swiglu_in(hb, ssq, gain, w_in, w_out, layer, tm=2048, tn=256):
    T, K = hb.shape
    tm = min(tm, T)
    F = w_in.shape[2] // 2
    N = w_out.shape[2]
    nf = F // tn
    steps = (T // tm) * nf
    slab = F // steps
    assert F % steps == 0 and slab % 16 == 0, (F, steps)
    gb = jnp.broadcast_to(gain.astype(F32)[:, None], (K, LANES))
    return pl.pallas_call(
        _swiglu_in_kernel,
        grid=(T // tm, nf),
        in_specs=[pl.BlockSpec((tm, K), lambda i, j: (i, 0)),
                  pl.BlockSpec((tm, LANES), lambda i, j: (i, 0)),
                  pl.BlockSpec((K, LANES), lambda i, j: (0, 0)),
                  pl.BlockSpec((None, K, tn), lambda i, j: (layer, 0, j)),
                  pl.BlockSpec((None, K, tn), lambda i, j: (layer, 0, j + nf)),
                  pl.BlockSpec((None, slab, N), lambda i, j: (layer, i * nf + j, 0))],
        out_specs=[pl.BlockSpec((tm, tn), lambda i, j: (i, j)),
                   pl.BlockSpec((slab, N), lambda i, j: (i * nf + j, 0))],
        out_shape=[jax.ShapeDtypeStruct((T, F), BF16), jax.ShapeDtypeStruct((F, N), BF16)],
        compiler_params=_params("parallel", "arbitrary", vmem_limit=VMEM_LIMIT_WIDE_BYTES),
        name="swiglu_in",
    )(hb, ssq, gb, w_in, w_in, w_out)


def _emit_stream(h, j, h_ref, hb_ref, ssq_ref):
    h_ref[...] = h
    hb_ref[...] = h.astype(hb_ref.dtype)
    part = _lane_partial_sumsq(h)

    @pl.when(j == 0)
    def _():
        ssq_ref[...] = part

    @pl.when(j != 0)
    def _():
        ssq_ref[...] += part


def _stream_out(T, N, tm, tn):
    specs = [pl.BlockSpec((tm, tn), lambda i, j: (i, j)),
             pl.BlockSpec((tm, tn), lambda i, j: (i, j)),
             pl.BlockSpec((tm, LANES), lambda i, j: (i, 0))]
    shapes = [jax.ShapeDtypeStruct((T, N), F32), jax.ShapeDtypeStruct((T, N), BF16),
              jax.ShapeDtypeStruct((T, LANES), F32)]
    return specs, shapes


def _ffn_out_kernel(a_ref, w_ref, r_ref, h_ref, hb_ref, ssq_ref, *, scale):
    h = r_ref[...] + scale * jnp.dot(a_ref[...], w_ref[...], preferred_element_type=F32)
    h_ref[...] = h
    hb_ref[...] = h.astype(hb_ref.dtype)
    ssq_ref[...] = _lane_partial_sumsq(h)


def ffn_out(a, w, resid, scale, tm=512, tn=1024):
    T, K = a.shape
    tm = min(tm, T)
    N = w.shape[1]
    tn = _tile(N, tn)
    h, hb, ssq = pl.pallas_call(
        functools.partial(_ffn_out_kernel, scale=scale),
        grid=(N // tn, T // tm),
        in_specs=[pl.BlockSpec((tm, K), lambda j, i: (i, 0)),
                  pl.BlockSpec((K, tn), lambda j, i: (0, j), pipeline_mode=pl.Buffered(1)),
                  pl.BlockSpec((tm, tn), lambda j, i: (i, j))],
        out_specs=[pl.BlockSpec((tm, tn), lambda j, i: (i, j)),
                   pl.BlockSpec((tm, tn), lambda j, i: (i, j)),
                   pl.BlockSpec((None, tm, LANES), lambda j, i: (j, i, 0))],
        out_shape=[jax.ShapeDtypeStruct((T, N), F32), jax.ShapeDtypeStruct((T, N), BF16),
                   jax.ShapeDtypeStruct((N // tn, T, LANES), F32)],
        compiler_params=_params("parallel", "parallel", vmem_limit=VMEM_LIMIT_WIDE_BYTES),
        name="ffn_out",
    )(a, w, resid)
    return h, hb, jnp.sum(ssq, axis=0)


def _mix_out_kernel(ya_ref, yb_ref, yc_ref, gb_ref, gc_ref, w_ref, r_ref, h_ref, hb_ref, ssq_ref, yn_ref):
    j = pl.program_id(1)
    ca, cb = ya_ref.shape[1], yb_ref.shape[1]

    @pl.when(j == 0)
    def _():
        def norm(y_ref, g_ref):
            y = y_ref[...].astype(F32)
            ms = jnp.mean(y * y, axis=-1, keepdims=True)
            return (y * lax.rsqrt(ms + EPS) * g_ref[...]).astype(yn_ref.dtype)
        yn_ref[:, 0:ca] = ya_ref[...]
        yn_ref[:, ca:ca + cb] = norm(yb_ref, gb_ref)
        yn_ref[:, ca + cb:] = norm(yc_ref, gc_ref)

    acc = jnp.dot(yn_ref[...], w_ref[...], preferred_element_type=F32)
    _emit_stream(r_ref[...] + acc, j, h_ref, hb_ref, ssq_ref)


def mix_out(ya, yb, yc, gain_b, gain_c, w, layer, resid, tm=1024, tn=512):
    T, ca = ya.shape
    tm = min(tm, T)
    cb, cc = yb.shape[1], yc.shape[1]
    K, N = w.shape[1], w.shape[2]
    tn = _tile(N, tn)
    out_specs, out_shape = _stream_out(T, N, tm, tn)
    rows = lambda c: pl.BlockSpec((tm, c), lambda i, j: (i, 0))
    gain = lambda c: pl.BlockSpec((1, c), lambda i, j: (0, 0))
    return pl.pallas_call(
        _mix_out_kernel,
        grid=(T // tm, N // tn),
        in_specs=[rows(ca), rows(cb), rows(cc), gain(cb), gain(cc),
                  pl.BlockSpec((None, K, tn), lambda i, j: (layer, 0, j)),
                  pl.BlockSpec((tm, tn), lambda i, j: (i, j))],
        out_specs=out_specs,
        out_shape=out_shape,
        scratch_shapes=[pltpu.VMEM((tm, K), BF16)],
        compiler_params=_params("parallel", "arbitrary", vmem_limit=VMEM_LIMIT_WIDE_BYTES),
        name="mix_out",
    )(ya, yb, yc, gain_b.reshape(1, cb).astype(F32), gain_c.reshape(1, cc).astype(F32), w, resid)


def _mix_prep_kernel(wt_ref, g_ref, o_ref, *, q_lo, q_hi, q_scale, transpose):
    i = pl.program_id(1)
    scale = jnp.where((i >= q_lo) & (i < q_hi), q_scale, 1.0).astype(F32)
    w = wt_ref[...] * (g_ref[...] * scale)
    o_ref[...] = (w.T if transpose else w).astype(o_ref.dtype)


def mix_prep(w_t, gain, row0, n_rows, q_rows, q_scale, transpose=True, tr=256):
    L, d_in, K = w_t.shape
    tr = _tile(n_rows, tr)
    assert d_in % SUBLANES == 0 and row0 % SUBLANES == 0 and q_rows[0] % tr == 0 and q_rows[1] % tr == 0
    assert row0 + n_rows <= d_in
    rows = lambda l, i: (pl.multiple_of(l * d_in + row0 + i * tr, SUBLANES), 0)
    if transpose:
        out_spec, out_dims = pl.BlockSpec((None, K, tr), lambda l, i: (l, 0, i)), (L, K, n_rows)
    else:
        out_spec, out_dims = pl.BlockSpec((None, tr, K), lambda l, i: (l, i, 0)), (L, n_rows, K)
    return pl.pallas_call(
        functools.partial(_mix_prep_kernel, q_lo=q_rows[0] // tr, q_hi=q_rows[1] // tr, q_scale=q_scale,
                          transpose=transpose),
        grid=(L, n_rows // tr),
        in_specs=[pl.BlockSpec((pl.Element(tr), pl.Element(K)), rows),
                  pl.BlockSpec((None, 1, K), lambda l, i: (l, 0, 0))],
        out_specs=out_spec,
        out_shape=jax.ShapeDtypeStruct(out_dims, BF16),
        compiler_params=_params("parallel", "parallel"),
        name="mix_prep",
    )(w_t.reshape(L * d_in, K), gain.astype(F32)[:, None, :])


def _scaled_matmul_kernel(a_ref, ssq_ref, w_ref, o_ref):
    rs = _row_scale(ssq_ref[...], a_ref.shape[1])
    o_ref[...] = (jnp.dot(a_ref[...], w_ref[...], preferred_element_type=F32) * rs).astype(o_ref.dtype)


def scaled_matmul(hb, ssq, w, layer, col0, N, out_dtype, tm=1024, tn=1024):
    T, K = hb.shape
    tm = min(tm, T)
    tn = _tile(math.gcd(N, col0), tn)
    j0 = col0 // tn
    return pl.pallas_call(
        _scaled_matmul_kernel,
        grid=(T // tm, N // tn),
        in_specs=[pl.BlockSpec((tm, K), lambda i, j: (i, 0)),
                  pl.BlockSpec((tm, LANES), lambda i, j: (i, 0)),
                  pl.BlockSpec((None, K, tn), lambda i, j: (layer, 0, j0 + j))],
        out_specs=pl.BlockSpec((tm, tn), lambda i, j: (i, j)),
        out_shape=jax.ShapeDtypeStruct((T, N), out_dtype),
        compiler_params=_params("parallel", "arbitrary"),
        name="scaled_matmul",
    )(hb, ssq, w)


def _row_iota(shape):
    return lax.broadcasted_iota(jnp.int32, shape, 0)


def _log_sigmoid(x):
    return jnp.minimum(x, 0.0) - jnp.log1p(jnp.exp(-jnp.abs(x)))


def _softplus(x):
    return jnp.maximum(x, 0.0) + jnp.log1p(jnp.exp(-jnp.abs(x)))


def _forget_cumsum_kernel(x_ref, ssq_ref, w_ref, b_ref, o_ref, carry_ref):
    @pl.when(pl.program_id(1) == 0)
    def _():
        carry_ref[...] = jnp.zeros_like(carry_ref)

    rs = _row_scale(ssq_ref[...], x_ref.shape[1])
    z = _dot_nt(x_ref[...], w_ref[...]) * rs + b_ref[...]
    c = _log_sigmoid(z)
    ts = c.shape[0]
    rows = _row_iota(c.shape)
    shift = 1
    while shift < ts:
        c = c + jnp.where(rows >= shift, pltpu.roll(c, shift, 0), 0.0)
        shift *= 2
    c = c + carry_ref[...]
    o_ref[...] = c
    carry_ref[...] = c[ts - 1:ts, :]


def forget_cumsum(hb, ssq, w_ff, b_ff, layer, batch, ts=512):
    T, K = hb.shape
    S = T // batch
    ts = min(ts, S)
    nc = S // ts
    P = w_ff.shape[1]
    return pl.pallas_call(
        _forget_cumsum_kernel,
        grid=(batch, nc),
        in_specs=[pl.BlockSpec((ts, K), lambda b, c: (b * nc + c, 0)),
                  pl.BlockSpec((ts, LANES), lambda b, c: (b * nc + c, 0)),
                  pl.BlockSpec((None, P, K), lambda b, c: (layer, 0, 0)),
                  pl.BlockSpec((None, 1, P), lambda b, c: (layer, 0, 0))],
        out_specs=pl.BlockSpec((ts, P), lambda b, c: (b * nc + c, 0)),
        out_shape=jax.ShapeDtypeStruct((T, P), F32),
        scratch_shapes=[pltpu.VMEM((1, P), F32)],
        compiler_params=_params("parallel", "arbitrary"),
        name="forget_cumsum",
    )(hb, ssq, w_ff, b_ff)


def _lru_kernel(xa_ref, ga_ref, cw_ref, cb_ref, wa_ref, ba_ref, wx_ref, bx_ref, lam_ref, gn_ref,
                o_ref, tail_ref, h_ref):
    @pl.when(pl.program_id(1) == 0)
    def _():
        tail_ref[...] = jnp.zeros_like(tail_ref)
        h_ref[...] = jnp.zeros_like(h_ref)

    x = xa_ref[...]
    ts, C = x.shape
    xb = jnp.concatenate([tail_ref[...], x], axis=0)
    y = cb_ref[...] + cw_ref[CONV_WIDTH - 1:CONV_WIDTH, :] * x
    for back in range(1, CONV_WIDTH):
        y = y + cw_ref[CONV_WIDTH - 1 - back:CONV_WIDTH - back, :] * pltpu.roll(xb, back, 0)[SUBLANES:, :]
    tail_ref[...] = x[ts - SUBLANES:, :]

    rs, gs = [], []
    for g in range(C // LRU_BLOCK):
        yg = y[:, g * LRU_BLOCK:(g + 1) * LRU_BLOCK].astype(BF16)
        rs.append(jnp.dot(yg, wa_ref[g], preferred_element_type=F32))
        gs.append(jnp.dot(yg, wx_ref[g], preferred_element_type=F32))
    r = jax.nn.sigmoid(jnp.concatenate(rs, axis=1) + ba_ref[...])
    i = jax.nn.sigmoid(jnp.concatenate(gs, axis=1) + bx_ref[...])

    log_a = (-LRU_C) * r * _softplus(-lam_ref[...])
    a = jnp.exp(log_a)
    th = jnp.abs(jnp.tanh(log_a))
    u = jnp.sqrt(2.0 * th / (1.0 + th)) * (i * y)

    n_groups = ts // SUBLANES
    a = a.reshape(n_groups, SUBLANES, C)
    u = u.reshape(n_groups, SUBLANES, C)
    sub = lax.broadcasted_iota(jnp.int32, a.shape, 1)
    shift = 1
    while shift < SUBLANES:
        keep = sub >= shift
        a_prev = jnp.where(keep, pltpu.roll(a, shift, 1), 1.0)
        u_prev = jnp.where(keep, pltpu.roll(u, shift, 1), 0.0)
        u = a * u_prev + u
        a = a * a_prev
        shift *= 2
    carry = h_ref[...]
    groups = []
    for g in range(n_groups):
        hg = a[g] * carry + u[g]
        groups.append(hg)
        carry = hg[SUBLANES - 1:SUBLANES, :]
    h = jnp.concatenate(groups, axis=0)
    h_ref[...] = carry

    out = h * jax.nn.gelu(ga_ref[...])
    ms = jnp.mean(out * out, axis=-1, keepdims=True)
    o_ref[...] = (out * lax.rsqrt(ms + EPS) * gn_ref[...]).astype(o_ref.dtype)


def lru_branch(proj, conv_w, conv_b, w_a, b_a, w_x, b_x, lam, gain, batch, ts=256):
    T = proj.shape[0]
    C = proj.shape[1] // 2
    S = T // batch
    ts = min(ts, S)
    nc = S // ts
    row = lambda v: v.reshape(1, C).astype(F32)
    full2 = lambda shape: pl.BlockSpec(shape, lambda b, c: (0, 0))
    full3 = lambda shape: pl.BlockSpec(shape, lambda b, c: (0, 0, 0))
    return pl.pallas_call(
        _lru_kernel,
        grid=(batch, nc),
        in_specs=[pl.BlockSpec((ts, C), lambda b, c: (b * nc + c, 0)),
                  pl.BlockSpec((ts, C), lambda b, c: (b * nc + c, 1)),
                  full2((CONV_WIDTH, C)), full2((1, C)),
                  full3(w_a.shape), full2((1, C)),
                  full3(w_x.shape), full2((1, C)),
                  full2((1, C)), full2((1, C))],
        out_specs=pl.BlockSpec((ts, C), lambda b, c: (b * nc + c, 0)),
        out_shape=jax.ShapeDtypeStruct((T, C), BF16),
        scratch_shapes=[pltpu.VMEM((SUBLANES, C), F32), pltpu.VMEM((1, C), F32)],
        compiler_params=_params("parallel", "arbitrary"),
        name="lru_branch",
    )(proj, proj, conv_w.astype(F32), row(conv_b), w_a.astype(BF16), row(b_a),
      w_x.astype(BF16), row(b_x), row(lam), row(gain))


def _softmax2(z_chunks, row_offset, p_dtype):
    m = z_chunks[0]
    for zc in z_chunks[1:]:
        m = jnp.maximum(m, zc)
    m = jnp.max(m, axis=-1, keepdims=True)
    shift = m if row_offset is None else (m + row_offset) - row_offset
    ps = [jnp.exp2(zc - shift) for zc in z_chunks]
    l = ps[0]
    for p in ps[1:]:
        l = l + p
    l = jnp.sum(l, axis=-1, keepdims=True)
    p = jnp.concatenate(ps, axis=1) if len(ps) > 1 else ps[0]
    return p.astype(p_dtype), l


def _causal_sweep(q_ref, k_ref, v_ref, o_ref, tq, bias_chunks, row_offset):
    nq = q_ref.shape[0] // tq
    order = list(range(nq))[::-1]

    def scores(i):
        return _dot_nt(q_ref[i * tq:(i + 1) * tq, :], k_ref[0:(i + 1) * tq, :])

    def probs(i, s):
        z = [bias_chunks(i, c, s[:, c * tq:(c + 1) * tq]) for c in range(i + 1)]
        return _softmax2(z, None if row_offset is None else row_offset(i), v_ref.dtype)

    def values(i, p, l):
        pv = jnp.dot(p, v_ref[0:(i + 1) * tq, :], preferred_element_type=F32)
        o_ref[i * tq:(i + 1) * tq, :] = (pv / l).astype(o_ref.dtype)

    s_next, pending = scores(order[0]), None
    for n, i in enumerate(order):
        s = s_next
        if n + 1 < nq:
            s_next = scores(order[n + 1])
        p, l = probs(i, s)
        if pending is not None:
            values(*pending)
        pending = (i, p, l)
    values(*pending)


def _fox_kernel(q_ref, k_ref, v_ref, cq_ref, ck_ref, tri_ref, o_ref, *, tq):
    ck2 = ck_ref[...] * LOG2E

    def bias_chunks(i, c, s):
        z = s - ck2[:, c * tq:(c + 1) * tq]
        return z + tri_ref[...] if c == i else z

    def row_offset(i):
        return cq_ref[i * tq:(i + 1) * tq, :] * LOG2E

    _causal_sweep(q_ref, k_ref, v_ref, o_ref, tq, bias_chunks, row_offset)


def fox_attention(qkv, col0, n_heads, cum_col, cum_row, batch, tq=256):
    T = qkv.shape[0]
    S = T // batch
    tq = min(tq, S)
    hd = HEAD_DIM
    tri = np.where(np.arange(tq)[:, None] >= np.arange(tq)[None, :], 0.0, NEG_BIG).astype(np.float32)
    qkv_spec = lambda off: pl.BlockSpec((S, hd), lambda b, h: (b, col0 + off + h))
    return pl.pallas_call(
        functools.partial(_fox_kernel, tq=tq),
        grid=(batch, n_heads),
        in_specs=[qkv_spec(0), qkv_spec(n_heads), qkv_spec(2 * n_heads),
                  pl.BlockSpec((None, None, S, 1), lambda b, h: (b, h, 0, 0)),
                  pl.BlockSpec((None, None, 1, S), lambda b, h: (b, h, 0, 0)),
                  pl.BlockSpec((tq, tq), lambda b, h: (0, 0))],
        out_specs=pl.BlockSpec((S, hd), lambda b, h: (b, h)),
        out_shape=jax.ShapeDtypeStruct((T, n_heads * hd), BF16),
        compiler_params=_params("parallel", "parallel"),
        name="fox_attention",
    )(qkv, qkv, qkv, cum_col, cum_row, jnp.asarray(tri))


def _dilated_kernel(q_ref, k_ref, v_ref, rc_ref, ra_ref, rb_ref, lm_ref, o_ref, qs_ref, ks_ref, *, tq):
    def rope(x_ref):
        x = x_ref[...].astype(F32)
        half = ROPE_DIM // 2
        return (x * rc_ref[...] + pltpu.roll(x, HEAD_DIM - half, 1) * ra_ref[...]
                + pltpu.roll(x, half, 1) * rb_ref[...])

    qs_ref[...] = rope(q_ref).astype(qs_ref.dtype)
    ks_ref[...] = rope(k_ref).astype(ks_ref.dtype)
    _causal_sweep(qs_ref, ks_ref, v_ref, o_ref, tq, lambda i, c, s: s + lm_ref[i - c], None)


def _dilated_log2_multiplicity(S, tq):
    nq = S // tq
    delta = (np.arange(nq)[:, None, None] * tq + np.arange(tq)[None, :, None]
             - np.arange(tq)[None, None, :])
    mult = np.zeros(delta.shape, np.float64)
    for window, dilation in DILATED_PATTERNS:
        mult += (delta >= 0) & (delta <= window) & (delta % dilation == 0)
    with np.errstate(divide="ignore"):
        return np.where(mult > 0, np.log2(mult), NEG_BIG).astype(np.float32)


def _rope_tables(S):
    inv = 1.0 / (ROPE_THETA ** (jnp.arange(0, ROPE_DIM, 2, dtype=F32) / ROPE_DIM))
    ang = jnp.arange(S, dtype=F32)[:, None] * inv[None, :]
    cos, sin = jnp.cos(ang), jnp.sin(ang)
    half = ROPE_DIM // 2
    pad = lambda t, lo: jnp.concatenate(
        [jnp.zeros((S, lo), F32), t, jnp.zeros((S, HEAD_DIM - lo - t.shape[1]), F32)], axis=1)
    rc = jnp.concatenate([cos, cos, jnp.ones((S, HEAD_DIM - ROPE_DIM), F32)], axis=1)
    ra = pad(-sin, 0)
    rb = pad(sin, half)
    return rc, ra, rb


def dilated_attention(qkv, col0, n_heads, batch, tq=256):
    T = qkv.shape[0]
    S = T // batch
    tq = min(tq, S)
    hd = HEAD_DIM
    nq = S // tq
    rc, ra, rb = _rope_tables(S)
    lm = jnp.asarray(_dilated_log2_multiplicity(S, tq))
    qkv_spec = lambda off: pl.BlockSpec((S, hd), lambda b, h: (b, col0 + off + h))
    tab = pl.BlockSpec((S, hd), lambda b, h: (0, 0))
    return pl.pallas_call(
        functools.partial(_dilated_kernel, tq=tq),
        grid=(batch, n_heads),
        in_specs=[qkv_spec(0), qkv_spec(n_heads), qkv_spec(2 * n_heads), tab, tab, tab,
                  pl.BlockSpec((nq, tq, tq), lambda b, h: (0, 0, 0))],
        out_specs=pl.BlockSpec((S, hd), lambda b, h: (b, h)),
        out_shape=jax.ShapeDtypeStruct((T, n_heads * hd), BF16),
        scratch_shapes=[pltpu.VMEM((S, hd), BF16), pltpu.VMEM((S, hd), BF16)],
        compiler_params=_params("parallel", "parallel"),
        name="dilated_attention",
    )(qkv, qkv, qkv, rc, ra, rb, lm)


def kernel(x, ffn1_norm, ffn1_w_in, ffn1_w_out, mix_norm, mix_w_in, conv_w, conv_b, lru_w_a, lru_b_a,
           lru_w_x, lru_b_x, lru_lam, fox_b_f, out_norm_lru, out_norm_fox, out_norm_dil, mix_w_out,
           ffn2_norm, ffn2_w_in, ffn2_w_out, final_norm):
    B, S, D = x.shape
    T = B * S
    depth = ffn1_norm.shape[0]
    d_lru = conv_w.shape[2]
    n_fox = fox_b_f.shape[1]
    d_fox = n_fox * HEAD_DIM
    d_dil = out_norm_dil.shape[1]
    n_dil = d_dil // HEAD_DIM

    mix_w_out_b = mix_w_out.astype(BF16)
    w_t = jnp.swapaxes(mix_w_in, 1, 2)
    q_scale = HEAD_DIM ** -0.5 * LOG2E
    ff0 = 2 * d_lru + 3 * d_fox
    w_head = mix_prep(w_t, mix_norm, 0, ff0, (2 * d_lru, 2 * d_lru + d_fox), q_scale)
    w_dil = mix_prep(w_t, mix_norm, ff0 + n_fox, 3 * d_dil, (0, d_dil), q_scale)
    w_ff = mix_prep(w_t, mix_norm, ff0, LANES, (0, 0), q_scale, transpose=False)
    b_ff = jnp.pad(fox_b_f.astype(F32), ((0, 0), (0, LANES - n_fox))).reshape(depth, 1, LANES)

    h = x.reshape(T, D)
    hb, ssq = rowstat(h)
    for l in range(depth):
        act, w_out_b = swiglu_in(hb, ssq, ffn1_norm[l], ffn1_w_in, ffn1_w_out, l)
        h, hb, ssq = ffn_out(act, w_out_b, h, 0.5)

        proj_lru = scaled_matmul(hb, ssq, w_head, l, 0, 2 * d_lru, F32)
        proj_fox = scaled_matmul(hb, ssq, w_head, l, 2 * d_lru, 3 * d_fox, BF16)
        proj_dil = scaled_matmul(hb, ssq, w_dil, l, 0, 3 * d_dil, BF16)
        cum = forget_cumsum(hb, ssq, w_ff, b_ff, l, B)[:, :n_fox].reshape(B, S, n_fox).transpose(0, 2, 1)
        y_a = lru_branch(proj_lru, conv_w[l], conv_b[l], lru_w_a[l], lru_b_a[l], lru_w_x[l], lru_b_x[l],
                         lru_lam[l], out_norm_lru[l], B)
        y_b = fox_attention(proj_fox, 0, n_fox, cum[:, :, :, None], cum[:, :, None, :], B)
        y_c = dilated_attention(proj_dil, 0, n_dil, B)
        h, hb, ssq = mix_out(y_a, y_b, y_c, out_norm_fox[l], out_norm_dil[l], mix_w_out_b, l, h)

        act, w_out_b = swiglu_in(hb, ssq, ffn2_norm[l], ffn2_w_in, ffn2_w_out, l)
        h, hb, ssq = ffn_out(act, w_out_b, h, 0.5)
    return scale_norm(h, ssq, final_norm).reshape(B, S, D)
```

```python
import functools
import math

import numpy as np
import jax
import jax.numpy as jnp
from jax import lax
from jax.experimental import pallas as pl
from jax.experimental.pallas import tpu as pltpu

F32 = jnp.float32
BF16 = jnp.bfloat16

LANES = 128
SUBLANES = 8
HEAD_DIM = 128
LRU_BLOCK = 128
CONV_WIDTH = 4
LRU_C = 8.0
ROPE_THETA = 500000.0
ROPE_DIM = HEAD_DIM // 4
DILATED_PATTERNS = ((128, 1), (512, 4), (2048, 16))
EPS = 1e-6
NEG_BIG = -1e30
LOG2E = 1.4426950408889634
SWIGLU_ROW_CHUNK = 256
HEADS_PER_STEP = 2

VMEM_BYTES_V7X = 64 * 1024 * 1024
VMEM_LIMIT_BYTES = VMEM_BYTES_V7X - 8 * 1024 * 1024
VMEM_LIMIT_WIDE_BYTES = VMEM_BYTES_V7X - 2 * 1024 * 1024


def _params(*sem, vmem_limit=None):
    return pltpu.CompilerParams(dimension_semantics=sem, vmem_limit_bytes=vmem_limit or VMEM_LIMIT_BYTES)


def _tile(n, preferred):
    t = min(preferred, n)
    while n % t:
        t //= 2
    assert t % LANES == 0 or t == n, (n, preferred)
    return t


def _lane_partial_sumsq(x):
    x2 = x * x
    acc = x2[:, 0:LANES]
    for k in range(1, x.shape[1] // LANES):
        acc = acc + x2[:, k * LANES:(k + 1) * LANES]
    return acc


def _dot_nt(a, b):
    return lax.dot_general(a, b, (((1,), (1,)), ((), ())), preferred_element_type=F32)


def _row_scale(ssq, width):
    return lax.rsqrt(jnp.sum(ssq, axis=-1, keepdims=True) * (1.0 / width) + EPS)


def _rowstat_kernel(x_ref, hb_ref, ssq_ref):
    x = x_ref[...]
    hb_ref[...] = x.astype(hb_ref.dtype)
    ssq_ref[...] = _lane_partial_sumsq(x)


def rowstat(x, tm=512):
    T, C = x.shape
    tm = min(tm, T)
    return pl.pallas_call(
        _rowstat_kernel,
        grid=(T // tm,),
        in_specs=[pl.BlockSpec((tm, C), lambda i: (i, 0))],
        out_specs=[pl.BlockSpec((tm, C), lambda i: (i, 0)), pl.BlockSpec((tm, LANES), lambda i: (i, 0))],
        out_shape=[jax.ShapeDtypeStruct((T, C), BF16), jax.ShapeDtypeStruct((T, LANES), F32)],
        compiler_params=_params("parallel"),
        name="rowstat",
    )(x)


def _scale_norm_kernel(x_ref, ssq_ref, g_ref, o_ref):
    x = x_ref[...]
    o_ref[...] = (x * _row_scale(ssq_ref[...], x.shape[1]) * g_ref[...]).astype(o_ref.dtype)


def scale_norm(x, ssq, g, tm=512):
    T, C = x.shape
    tm = min(tm, T)
    return pl.pallas_call(
        _scale_norm_kernel,
        grid=(T // tm,),
        in_specs=[pl.BlockSpec((tm, C), lambda i: (i, 0)),
                  pl.BlockSpec((tm, LANES), lambda i: (i, 0)),
                  pl.BlockSpec((1, C), lambda i: (0, 0))],
        out_specs=pl.BlockSpec((tm, C), lambda i: (i, 0)),
        out_shape=jax.ShapeDtypeStruct((T, C), x.dtype),
        compiler_params=_params("parallel"),
        name="scale_norm",
    )(x, ssq, g.reshape(1, C).astype(F32))


def _swiglu_in_kernel(x_ref, ssq_ref, gb_ref, wg_ref, wu_ref, wo_ref, o_ref, wob_ref):
    K, tn = wg_ref.shape
    gb = jnp.concatenate([gb_ref[...]] * (tn // LANES), axis=1)
    wg = (wg_ref[...] * gb).astype(BF16)
    wu = (wu_ref[...] * gb).astype(BF16)
    rc = min(SWIGLU_ROW_CHUNK, x_ref.shape[0])
    for r in range(x_ref.shape[0] // rc):
        rows = slice(r * rc, (r + 1) * rc)
        x = x_ref[rows, :]
        rs = _row_scale(ssq_ref[rows, :], K)
        g = jnp.dot(x, wg, preferred_element_type=F32) * rs
        u = jnp.dot(x, wu, preferred_element_type=F32) * rs
        o_ref[rows, :] = (g * jax.nn.sigmoid(g) * u).astype(o_ref.dtype)
    wob_ref[...] = wo_ref[...].astype(wob_ref.dtype)


def swiglu_in(hb, ssq, gain, w_in, w_out, layer, tm=2048, tn=256):
    T, K = hb.shape
    tm = min(tm, T)
    F = w_in.shape[2] // 2
    N = w_out.shape[2]
    nf = F // tn
    steps = (T // tm) * nf
    slab = F // steps
    assert F % steps == 0 and slab % 16 == 0, (F, steps)
    gb = jnp.broadcast_to(gain.astype(F32)[:, None], (K, LANES))
    return pl.pallas_call(
        _swiglu_in_kernel,
        grid=(T // tm, nf),
        in_specs=[pl.BlockSpec((tm, K), lambda i, j: (i, 0)),
                  pl.BlockSpec((tm, LANES), lambda i, j: (i, 0)),
                  pl.BlockSpec((K, LANES), lambda i, j: (0, 0)),
                  pl.BlockSpec((None, K, tn), lambda i, j: (layer, 0, j)),
                  pl.BlockSpec((None, K, tn), lambda i, j: (layer, 0, j + nf)),
                  pl.BlockSpec((None, slab, N), lambda i, j: (layer, i * nf + j, 0))],
        out_specs=[pl.BlockSpec((tm, tn), lambda i, j: (i, j)),
                   pl.BlockSpec((slab, N), lambda i, j: (i * nf + j, 0))],
        out_shape=[jax.ShapeDtypeStruct((T, F), BF16), jax.ShapeDtypeStruct((F, N), BF16)],
        compiler_params=_params("parallel", "arbitrary", vmem_limit=VMEM_LIMIT_WIDE_BYTES),
        name="swiglu_in",
    )(hb, ssq, gb, w_in, w_in, w_out)


def _emit_stream(h, j, h_ref, hb_ref, ssq_ref):
    h_ref[...] = h
    hb_ref[...] = h.astype(hb_ref.dtype)
    part = _lane_partial_sumsq(h)

    @pl.when(j == 0)
    def _():
        ssq_ref[...] = part

    @pl.when(j != 0)
    def _():
        ssq_ref[...] += part


def _stream_out(T, N, tm, tn):
    specs = [pl.BlockSpec((tm, tn), lambda i, j: (i, j)),
             pl.BlockSpec((tm, tn), lambda i, j: (i, j)),
             pl.BlockSpec((tm, LANES), lambda i, j: (i, 0))]
    shapes = [jax.ShapeDtypeStruct((T, N), F32), jax.ShapeDtypeStruct((T, N), BF16),
              jax.ShapeDtypeStruct((T, LANES), F32)]
    return specs, shapes


def _ffn_out_kernel(a_ref, w_ref, r_ref, h_ref, hb_ref, ssq_ref, *, scale):
    h = r_ref[...] + scale * jnp.dot(a_ref[...], w_ref[...], preferred_element_type=F32)
    h_ref[...] = h
    hb_ref[...] = h.astype(hb_ref.dtype)
    ssq_ref[...] = _lane_partial_sumsq(h)


def ffn_out(a, w, resid, scale, tm=512, tn=1024):
    T, K = a.shape
    tm = min(tm, T)
    N = w.shape[1]
    tn = _tile(N, tn)
    h, hb, ssq = pl.pallas_call(
        functools.partial(_ffn_out_kernel, scale=scale),
        grid=(N // tn, T // tm),
        in_specs=[pl.BlockSpec((tm, K), lambda j, i: (i, 0)),
                  pl.BlockSpec((K, tn), lambda j, i: (0, j), pipeline_mode=pl.Buffered(1)),
                  pl.BlockSpec((tm, tn), lambda j, i: (i, j))],
        out_specs=[pl.BlockSpec((tm, tn), lambda j, i: (i, j)),
                   pl.BlockSpec((tm, tn), lambda j, i: (i, j)),
                   pl.BlockSpec((None, tm, LANES), lambda j, i: (j, i, 0))],
        out_shape=[jax.ShapeDtypeStruct((T, N), F32), jax.ShapeDtypeStruct((T, N), BF16),
                   jax.ShapeDtypeStruct((N // tn, T, LANES), F32)],
        compiler_params=_params("parallel", "parallel", vmem_limit=VMEM_LIMIT_WIDE_BYTES),
        name="ffn_out",
    )(a, w, resid)
    return h, hb, jnp.sum(ssq, axis=0)


def _mix_out_kernel(ya_ref, yb_ref, yc_ref, gb_ref, gc_ref, w_ref, r_ref, h_ref, hb_ref, ssq_ref, yn_ref):
    j = pl.program_id(1)
    ca, cb = ya_ref.shape[1], yb_ref.shape[1]

    @pl.when(j == 0)
    def _():
        def norm(y_ref, g_ref):
            y = y_ref[...].astype(F32)
            ms = jnp.mean(y * y, axis=-1, keepdims=True)
            return (y * lax.rsqrt(ms + EPS) * g_ref[...]).astype(yn_ref.dtype)
        yn_ref[:, 0:ca] = ya_ref[...]
        yn_ref[:, ca:ca + cb] = norm(yb_ref, gb_ref)
        yn_ref[:, ca + cb:] = norm(yc_ref, gc_ref)

    acc = jnp.dot(yn_ref[...], w_ref[...], preferred_element_type=F32)
    _emit_stream(r_ref[...] + acc, j, h_ref, hb_ref, ssq_ref)


def mix_out(ya, yb, yc, gain_b, gain_c, w, layer, resid, tm=1024, tn=512):
    T, ca = ya.shape
    tm = min(tm, T)
    cb, cc = yb.shape[1], yc.shape[1]
    K, N = w.shape[1], w.shape[2]
    tn = _tile(N, tn)
    out_specs, out_shape = _stream_out(T, N, tm, tn)
    rows = lambda c: pl.BlockSpec((tm, c), lambda i, j: (i, 0))
    gain = lambda c: pl.BlockSpec((1, c), lambda i, j: (0, 0))
    return pl.pallas_call(
        _mix_out_kernel,
        grid=(T // tm, N // tn),
        in_specs=[rows(ca), rows(cb), rows(cc), gain(cb), gain(cc),
                  pl.BlockSpec((None, K, tn), lambda i, j: (layer, 0, j)),
                  pl.BlockSpec((tm, tn), lambda i, j: (i, j))],
        out_specs=out_specs,
        out_shape=out_shape,
        scratch_shapes=[pltpu.VMEM((tm, K), BF16)],
        compiler_params=_params("parallel", "arbitrary", vmem_limit=VMEM_LIMIT_WIDE_BYTES),
        name="mix_out",
    )(ya, yb, yc, gain_b.reshape(1, cb).astype(F32), gain_c.reshape(1, cc).astype(F32), w, resid)


def _mix_prep_kernel(wt_ref, g_ref, o_ref, *, q_lo, q_hi, q_scale, transpose):
    i = pl.program_id(1)
    scale = jnp.where((i >= q_lo) & (i < q_hi), q_scale, 1.0).astype(F32)
    w = wt_ref[...] * (g_ref[...] * scale)
    o_ref[...] = (w.T if transpose else w).astype(o_ref.dtype)


def mix_prep(w_t, gain, row0, n_rows, q_rows, q_scale, transpose=True, tr=256):
    L, d_in, K = w_t.shape
    tr = _tile(n_rows, tr)
    assert d_in % SUBLANES == 0 and row0 % SUBLANES == 0 and q_rows[0] % tr == 0 and q_rows[1] % tr == 0
    assert row0 + n_rows <= d_in
    rows = lambda l, i: (pl.multiple_of(l * d_in + row0 + i * tr, SUBLANES), 0)
    if transpose:
        out_spec, out_dims = pl.BlockSpec((None, K, tr), lambda l, i: (l, 0, i)), (L, K, n_rows)
    else:
        out_spec, out_dims = pl.BlockSpec((None, tr, K), lambda l, i: (l, i, 0)), (L, n_rows, K)
    return pl.pallas_call(
        functools.partial(_mix_prep_kernel, q_lo=q_rows[0] // tr, q_hi=q_rows[1] // tr, q_scale=q_scale,
                          transpose=transpose),
        grid=(L, n_rows // tr),
        in_specs=[pl.BlockSpec((pl.Element(tr), pl.Element(K)), rows),
                  pl.BlockSpec((None, 1, K), lambda l, i: (l, 0, 0))],
        out_specs=out_spec,
        out_shape=jax.ShapeDtypeStruct(out_dims, BF16),
        compiler_params=_params("parallel", "parallel"),
        name="mix_prep",
    )(w_t.reshape(L * d_in, K), gain.astype(F32)[:, None, :])


def _scaled_matmul_kernel(a_ref, ssq_ref, w_ref, o_ref):
    rs = _row_scale(ssq_ref[...], a_ref.shape[1])
    o_ref[...] = (jnp.dot(a_ref[...], w_ref[...], preferred_element_type=F32) * rs).astype(o_ref.dtype)


def scaled_matmul(hb, ssq, w, layer, col0, N, out_dtype, tm=1024, tn=1024):
    T, K = hb.shape
    tm = min(tm, T)
    tn = _tile(math.gcd(N, col0), tn)
    j0 = col0 // tn
    return pl.pallas_call(
        _scaled_matmul_kernel,
        grid=(T // tm, N // tn),
        in_specs=[pl.BlockSpec((tm, K), lambda i, j: (i, 0)),
                  pl.BlockSpec((tm, LANES), lambda i, j: (i, 0)),
                  pl.BlockSpec((None, K, tn), lambda i, j: (layer, 0, j0 + j))],
        out_specs=pl.BlockSpec((tm, tn), lambda i, j: (i, j)),
        out_shape=jax.ShapeDtypeStruct((T, N), out_dtype),
        compiler_params=_params("parallel", "arbitrary"),
        name="scaled_matmul",
    )(hb, ssq, w)


def _row_iota(shape):
    return lax.broadcasted_iota(jnp.int32, shape, 0)


def _log_sigmoid(x):
    return jnp.minimum(x, 0.0) - jnp.log1p(jnp.exp(-jnp.abs(x)))


def _softplus(x):
    return jnp.maximum(x, 0.0) + jnp.log1p(jnp.exp(-jnp.abs(x)))


def _forget_cumsum_kernel(x_ref, ssq_ref, w_ref, b_ref, o_ref, carry_ref):
    @pl.when(pl.program_id(1) == 0)
    def _():
        carry_ref[...] = jnp.zeros_like(carry_ref)

    rs = _row_scale(ssq_ref[...], x_ref.shape[1])
    z = _dot_nt(x_ref[...], w_ref[...]) * rs + b_ref[...]
    c = _log_sigmoid(z)
    ts = c.shape[0]
    rows = _row_iota(c.shape)
    shift = 1
    while shift < ts:
        c = c + jnp.where(rows >= shift, pltpu.roll(c, shift, 0), 0.0)
        shift *= 2
    c = c + carry_ref[...]
    o_ref[...] = c
    carry_ref[...] = c[ts - 1:ts, :]


def forget_cumsum(hb, ssq, w_ff, b_ff, layer, batch, ts=512):
    T, K = hb.shape
    S = T // batch
    ts = min(ts, S)
    nc = S // ts
    P = w_ff.shape[1]
    return pl.pallas_call(
        _forget_cumsum_kernel,
        grid=(batch, nc),
        in_specs=[pl.BlockSpec((ts, K), lambda b, c: (b * nc + c, 0)),
                  pl.BlockSpec((ts, LANES), lambda b, c: (b * nc + c, 0)),
                  pl.BlockSpec((None, P, K), lambda b, c: (layer, 0, 0)),
                  pl.BlockSpec((None, 1, P), lambda b, c: (layer, 0, 0))],
        out_specs=pl.BlockSpec((ts, P), lambda b, c: (b * nc + c, 0)),
        out_shape=jax.ShapeDtypeStruct((T, P), F32),
        scratch_shapes=[pltpu.VMEM((1, P), F32)],
        compiler_params=_params("parallel", "arbitrary"),
        name="forget_cumsum",
    )(hb, ssq, w_ff, b_ff)


def _lru_kernel(xa_ref, ga_ref, cw_ref, cb_ref, wa_ref, ba_ref, wx_ref, bx_ref, lam_ref, gn_ref,
                o_ref, tail_ref, h_ref):
    @pl.when(pl.program_id(1) == 0)
    def _():
        tail_ref[...] = jnp.zeros_like(tail_ref)
        h_ref[...] = jnp.zeros_like(h_ref)

    x = xa_ref[...]
    ts, C = x.shape
    xb = jnp.concatenate([tail_ref[...], x], axis=0)
    y = cb_ref[...] + cw_ref[CONV_WIDTH - 1:CONV_WIDTH, :] * x
    for back in range(1, CONV_WIDTH):
        y = y + cw_ref[CONV_WIDTH - 1 - back:CONV_WIDTH - back, :] * pltpu.roll(xb, back, 0)[SUBLANES:, :]
    tail_ref[...] = x[ts - SUBLANES:, :]

    rs, gs = [], []
    for g in range(C // LRU_BLOCK):
        yg = y[:, g * LRU_BLOCK:(g + 1) * LRU_BLOCK].astype(BF16)
        rs.append(jnp.dot(yg, wa_ref[g], preferred_element_type=F32))
        gs.append(jnp.dot(yg, wx_ref[g], preferred_element_type=F32))
    r = jax.nn.sigmoid(jnp.concatenate(rs, axis=1) + ba_ref[...])
    i = jax.nn.sigmoid(jnp.concatenate(gs, axis=1) + bx_ref[...])

    log_a = (-LRU_C) * r * _softplus(-lam_ref[...])
    a = jnp.exp(log_a)
    th = jnp.abs(jnp.tanh(log_a))
    u = jnp.sqrt(2.0 * th / (1.0 + th)) * (i * y)

    n_groups = ts // SUBLANES
    a = a.reshape(n_groups, SUBLANES, C)
    u = u.reshape(n_groups, SUBLANES, C)
    sub = lax.broadcasted_iota(jnp.int32, a.shape, 1)
    shift = 1
    while shift < SUBLANES:
        keep = sub >= shift
        a_prev = jnp.where(keep, pltpu.roll(a, shift, 1), 1.0)
        u_prev = jnp.where(keep, pltpu.roll(u, shift, 1), 0.0)
        u = a * u_prev + u
        a = a * a_prev
        shift *= 2
    carry = h_ref[...]
    groups = []
    for g in range(n_groups):
        hg = a[g] * carry + u[g]
        groups.append(hg)
        carry = hg[SUBLANES - 1:SUBLANES, :]
    h = jnp.concatenate(groups, axis=0)
    h_ref[...] = carry

    out = h * jax.nn.gelu(ga_ref[...])
    ms = jnp.mean(out * out, axis=-1, keepdims=True)
    o_ref[...] = (out * lax.rsqrt(ms + EPS) * gn_ref[...]).astype(o_ref.dtype)


def lru_branch(proj, conv_w, conv_b, w_a, b_a, w_x, b_x, lam, gain, batch, ts=256):
    T = proj.shape[0]
    C = proj.shape[1] // 2
    S = T // batch
    ts = min(ts, S)
    nc = S // ts
    row = lambda v: v.reshape(1, C).astype(F32)
    full2 = lambda shape: pl.BlockSpec(shape, lambda b, c: (0, 0))
    full3 = lambda shape: pl.BlockSpec(shape, lambda b, c: (0, 0, 0))
    return pl.pallas_call(
        _lru_kernel,
        grid=(batch, nc),
        in_specs=[pl.BlockSpec((ts, C), lambda b, c: (b * nc + c, 0)),
                  pl.BlockSpec((ts, C), lambda b, c: (b * nc + c, 1)),
                  full2((CONV_WIDTH, C)), full2((1, C)),
                  full3(w_a.shape), full2((1, C)),
                  full3(w_x.shape), full2((1, C)),
                  full2((1, C)), full2((1, C))],
        out_specs=pl.BlockSpec((ts, C), lambda b, c: (b * nc + c, 0)),
        out_shape=jax.ShapeDtypeStruct((T, C), BF16),
        scratch_shapes=[pltpu.VMEM((SUBLANES, C), F32), pltpu.VMEM((1, C), F32)],
        compiler_params=_params("parallel", "arbitrary"),
        name="lru_branch",
    )(proj, proj, conv_w.astype(F32), row(conv_b), w_a.astype(BF16), row(b_a),
      w_x.astype(BF16), row(b_x), row(lam), row(gain))


def _softmax2(z_chunks, row_offset, p_dtype):
    m = z_chunks[0]
    for zc in z_chunks[1:]:
        m = jnp.maximum(m, zc)
    m = jnp.max(m, axis=-1, keepdims=True)
    shift = m if row_offset is None else (m + row_offset) - row_offset
    ps = [jnp.exp2(zc - shift) for zc in z_chunks]
    l = ps[0]
    for p in ps[1:]:
        l = l + p
    l = jnp.sum(l, axis=-1, keepdims=True)
    p = jnp.concatenate(ps, axis=1) if len(ps) > 1 else ps[0]
    return p.astype(p_dtype), l


def _causal_sweep(heads, tq):
    nq = heads[0][0].shape[0] // tq
    items = [(i, head) for i in reversed(range(nq)) for head in heads]

    def scores(i, head):
        q_ref, k_ref = head[0], head[1]
        return _dot_nt(q_ref[i * tq:(i + 1) * tq, :], k_ref[0:(i + 1) * tq, :])

    def probs(i, head, s):
        v_ref, bias_chunks, row_offset = head[2], head[4], head[5]
        z = [bias_chunks(i, c, s[:, c * tq:(c + 1) * tq]) for c in range(i + 1)]
        return _softmax2(z, None if row_offset is None else row_offset(i), v_ref.dtype)

    def values(i, head, p, l):
        v_ref, o_ref = head[2], head[3]
        pv = jnp.dot(p, v_ref[0:(i + 1) * tq, :], preferred_element_type=F32)
        o_ref[i * tq:(i + 1) * tq, :] = (pv / l).astype(o_ref.dtype)

    s_next, pending = scores(*items[0]), None
    for n, item in enumerate(items):
        s = s_next
        if n + 1 < len(items):
            s_next = scores(*items[n + 1])
        p, l = probs(*item, s)
        if pending is not None:
            values(*pending)
        pending = item + (p, l)
    values(*pending)


def _fox_kernel(q_ref, k_ref, v_ref, cq_ref, ck_ref, tri_ref, o_ref, *, tq):
    heads = []
    for h in range(cq_ref.shape[0]):
        cols = pl.ds(h * HEAD_DIM, HEAD_DIM)
        ck2 = ck_ref[h] * LOG2E

        def bias_chunks(i, c, s, ck2=ck2):
            z = s - ck2[:, c * tq:(c + 1) * tq]
            return z + tri_ref[...] if c == i else z

        def row_offset(i, h=h):
            return cq_ref[h, i * tq:(i + 1) * tq, :] * LOG2E

        heads.append((q_ref.at[:, cols], k_ref.at[:, cols], v_ref.at[:, cols], o_ref.at[:, cols],
                      bias_chunks, row_offset))
    _causal_sweep(heads, tq)


def fox_attention(qkv, col0, n_heads, cum_col, cum_row, batch, tq=256):
    T = qkv.shape[0]
    S = T // batch
    tq = min(tq, S)
    hp = math.gcd(HEADS_PER_STEP, n_heads)
    w = hp * HEAD_DIM
    tri = np.where(np.arange(tq)[:, None] >= np.arange(tq)[None, :], 0.0, NEG_BIG).astype(np.float32)
    assert col0 % hp == 0
    qkv_spec = lambda off: pl.BlockSpec((S, w), lambda b, g: (b, (col0 + off) // hp + g))
    return pl.pallas_call(
        functools.partial(_fox_kernel, tq=tq),
        grid=(batch, n_heads // hp),
        in_specs=[qkv_spec(0), qkv_spec(n_heads), qkv_spec(2 * n_heads),
                  pl.BlockSpec((None, hp, S, 1), lambda b, g: (b, g, 0, 0)),
                  pl.BlockSpec((None, hp, 1, S), lambda b, g: (b, g, 0, 0)),
                  pl.BlockSpec((tq, tq), lambda b, g: (0, 0))],
        out_specs=pl.BlockSpec((S, w), lambda b, g: (b, g)),
        out_shape=jax.ShapeDtypeStruct((T, n_heads * HEAD_DIM), BF16),
        compiler_params=_params("parallel", "parallel"),
        name="fox_attention",
    )(qkv, qkv, qkv, cum_col, cum_row, jnp.asarray(tri))


def _dilated_kernel(q_ref, k_ref, v_ref, rc_ref, ra_ref, rb_ref, lm_ref, o_ref, qs_ref, ks_ref, *, tq):
    def rope(x):
        x = x.astype(F32)
        half = ROPE_DIM // 2
        return (x * rc_ref[...] + pltpu.roll(x, HEAD_DIM - half, 1) * ra_ref[...]
                + pltpu.roll(x, half, 1) * rb_ref[...])

    heads = [pl.ds(h * HEAD_DIM, HEAD_DIM) for h in range(q_ref.shape[1] // HEAD_DIM)]
    for cols in heads:
        qs_ref[:, cols] = rope(q_ref[:, cols]).astype(qs_ref.dtype)
        ks_ref[:, cols] = rope(k_ref[:, cols]).astype(ks_ref.dtype)
    add_log_multiplicity = lambda i, c, s: s + lm_ref[i - c]
    _causal_sweep([(qs_ref.at[:, cols], ks_ref.at[:, cols], v_ref.at[:, cols], o_ref.at[:, cols],
                    add_log_multiplicity, None) for cols in heads], tq)


def _dilated_log2_multiplicity(S, tq):
    nq = S // tq
    delta = (np.arange(nq)[:, None, None] * tq + np.arange(tq)[None, :, None]
             - np.arange(tq)[None, None, :])
    mult = np.zeros(delta.shape, np.float64)
    for window, dilation in DILATED_PATTERNS:
        mult += (delta >= 0) & (delta <= window) & (delta % dilation == 0)
    with np.errstate(divide="ignore"):
        return np.where(mult > 0, np.log2(mult), NEG_BIG).astype(np.float32)


def _rope_tables(S):
    inv = 1.0 / (ROPE_THETA ** (jnp.arange(0, ROPE_DIM, 2, dtype=F32) / ROPE_DIM))
    ang = jnp.arange(S, dtype=F32)[:, None] * inv[None, :]
    cos, sin = jnp.cos(ang), jnp.sin(ang)
    half = ROPE_DIM // 2
    pad = lambda t, lo: jnp.concatenate(
        [jnp.zeros((S, lo), F32), t, jnp.zeros((S, HEAD_DIM - lo - t.shape[1]), F32)], axis=1)
    rc = jnp.concatenate([cos, cos, jnp.ones((S, HEAD_DIM - ROPE_DIM), F32)], axis=1)
    ra = pad(-sin, 0)
    rb = pad(sin, half)
    return rc, ra, rb


def dilated_attention(qkv, col0, n_heads, batch, tq=256):
    T = qkv.shape[0]
    S = T // batch
    tq = min(tq, S)
    hd = HEAD_DIM
    nq = S // tq
    hp = math.gcd(HEADS_PER_STEP, n_heads)
    w = hp * hd
    assert col0 % hp == 0
    rc, ra, rb = _rope_tables(S)
    lm = jnp.asarray(_dilated_log2_multiplicity(S, tq))
    qkv_spec = lambda off: pl.BlockSpec((S, w), lambda b, g: (b, (col0 + off) // hp + g))
    tab = pl.BlockSpec((S, hd), lambda b, g: (0, 0))
    return pl.pallas_call(
        functools.partial(_dilated_kernel, tq=tq),
        grid=(batch, n_heads // hp),
        in_specs=[qkv_spec(0), qkv_spec(n_heads), qkv_spec(2 * n_heads), tab, tab, tab,
                  pl.BlockSpec((nq, tq, tq), lambda b, g: (0, 0, 0))],
        out_specs=pl.BlockSpec((S, w), lambda b, g: (b, g)),
        out_shape=jax.ShapeDtypeStruct((T, n_heads * hd), BF16),
        scratch_shapes=[pltpu.VMEM((S, w), BF16), pltpu.VMEM((S, w), BF16)],
        compiler_params=_params("parallel", "parallel"),
        name="dilated_attention",
    )(qkv, qkv, qkv, rc, ra, rb, lm)


def kernel(x, ffn1_norm, ffn1_w_in, ffn1_w_out, mix_norm, mix_w_in, conv_w, conv_b, lru_w_a, lru_b_a,
           lru_w_x, lru_b_x, lru_lam, fox_b_f, out_norm_lru, out_norm_fox, out_norm_dil, mix_w_out,
           ffn2_norm, ffn2_w_in, ffn2_w_out, final_norm):
    B, S, D = x.shape
    T = B * S
    depth = ffn1_norm.shape[0]
    d_lru = conv_w.shape[2]
    n_fox = fox_b_f.shape[1]
    d_fox = n_fox * HEAD_DIM
    d_dil = out_norm_dil.shape[1]
    n_dil = d_dil // HEAD_DIM

    mix_w_out_b = mix_w_out.astype(BF16)
    w_t = jnp.swapaxes(mix_w_in, 1, 2)
    q_scale = HEAD_DIM ** -0.5 * LOG2E
    ff0 = 2 * d_lru + 3 * d_fox
    w_head = mix_prep(w_t, mix_norm, 0, ff0, (2 * d_lru, 2 * d_lru + d_fox), q_scale)
    w_dil = mix_prep(w_t, mix_norm, ff0 + n_fox, 3 * d_dil, (0, d_dil), q_scale)
    w_ff = mix_prep(w_t, mix_norm, ff0, LANES, (0, 0), q_scale, transpose=False)
    b_ff = jnp.pad(fox_b_f.astype(F32), ((0, 0), (0, LANES - n_fox))).reshape(depth, 1, LANES)

    h = x.reshape(T, D)
    hb, ssq = rowstat(h)
    for l in range(depth):
        act, w_out_b = swiglu_in(hb, ssq, ffn1_norm[l], ffn1_w_in, ffn1_w_out, l)
        h, hb, ssq = ffn_out(act, w_out_b, h, 0.5)

        proj_lru = scaled_matmul(hb, ssq, w_head, l, 0, 2 * d_lru, F32)
        proj_fox = scaled_matmul(hb, ssq, w_head, l, 2 * d_lru, 3 * d_fox, BF16)
        proj_dil = scaled_matmul(hb, ssq, w_dil, l, 0, 3 * d_dil, BF16)
        cum = forget_cumsum(hb, ssq, w_ff, b_ff, l, B)[:, :n_fox].reshape(B, S, n_fox).transpose(0, 2, 1)
        y_a = lru_branch(proj_lru, conv_w[l], conv_b[l], lru_w_a[l], lru_b_a[l], lru_w_x[l], lru_b_x[l],
                         lru_lam[l], out_norm_lru[l], B)
        y_b = fox_attention(proj_fox, 0, n_fox, cum[:, :, :, None], cum[:, :, None, :], B)
        y_c = dilated_attention(proj_dil, 0, n_dil, B)
        h, hb, ssq = mix_out(y_a, y_b, y_c, out_norm_fox[l], out_norm_dil[l], mix_w_out_b, l, h)

        act, w_out_b = swiglu_in(hb, ssq, ffn2_norm[l], ffn2_w_in, ffn2_w_out, l)
        h, hb, ssq = ffn_out(act, w_out_b, h, 0.5)
    return scale_norm(h, ssq, final_norm).reshape(B, S, D)
```

```python
import functools
import math

import numpy as np
import jax
import jax.numpy as jnp
from jax import lax
from jax.experimental import pallas as pl
from jax.experimental.pallas import tpu as pltpu

F32 = jnp.float32
BF16 = jnp.bfloat16

LANES = 128
SUBLANES = 8
HEAD_DIM = 128
LRU_BLOCK = 128
CONV_WIDTH = 4
LRU_C = 8.0
ROPE_THETA = 500000.0
ROPE_DIM = HEAD_DIM // 4
DILATED_PATTERNS = ((128, 1), (512, 4), (2048, 16))
EPS = 1e-6
NEG_BIG = -1e30
LOG2E = 1.4426950408889634
SWIGLU_ROW_CHUNK = 256
HEADS_PER_STEP = 2

VMEM_BYTES_V7X = 64 * 1024 * 1024
VMEM_LIMIT_BYTES = VMEM_BYTES_V7X - 8 * 1024 * 1024
VMEM_LIMIT_WIDE_BYTES = VMEM_BYTES_V7X - 2 * 1024 * 1024


def _params(*sem, vmem_limit=None):
    return pltpu.CompilerParams(dimension_semantics=sem, vmem_limit_bytes=vmem_limit or VMEM_LIMIT_BYTES)


def _tile(n, preferred):
    t = min(preferred, n)
    while n % t:
        t //= 2
    assert t % LANES == 0 or t == n, (n, preferred)
    return t


def _lane_partial_sumsq(x):
    x2 = x * x
    acc = x2[:, 0:LANES]
    for k in range(1, x.shape[1] // LANES):
        acc = acc + x2[:, k * LANES:(k + 1) * LANES]
    return acc


def _dot_nt(a, b):
    return lax.dot_general(a, b, (((1,), (1,)), ((), ())), preferred_element_type=F32)


def _row_scale(ssq, width):
    return lax.rsqrt(jnp.sum(ssq, axis=-1, keepdims=True) * (1.0 / width) + EPS)


def _rowstat_kernel(x_ref, hb_ref, ssq_ref):
    x = x_ref[...]
    hb_ref[...] = x.astype(hb_ref.dtype)
    ssq_ref[...] = _lane_partial_sumsq(x)


def rowstat(x, tm=512):
    T, C = x.shape
    tm = min(tm, T)
    return pl.pallas_call(
        _rowstat_kernel,
        grid=(T // tm,),
        in_specs=[pl.BlockSpec((tm, C), lambda i: (i, 0))],
        out_specs=[pl.BlockSpec((tm, C), lambda i: (i, 0)), pl.BlockSpec((tm, LANES), lambda i: (i, 0))],
        out_shape=[jax.ShapeDtypeStruct((T, C), BF16), jax.ShapeDtypeStruct((T, LANES), F32)],
        compiler_params=_params("parallel"),
        name="rowstat",
    )(x)


def _scale_norm_kernel(x_ref, ssq_ref, g_ref, o_ref):
    x = x_ref[...]
    o_ref[...] = (x * _row_scale(ssq_ref[...], x.shape[1]) * g_ref[...]).astype(o_ref.dtype)


def scale_norm(x, ssq, g, tm=512):
    T, C = x.shape
    tm = min(tm, T)
    return pl.pallas_call(
        _scale_norm_kernel,
        grid=(T // tm,),
        in_specs=[pl.BlockSpec((tm, C), lambda i: (i, 0)),
                  pl.BlockSpec((tm, LANES), lambda i: (i, 0)),
                  pl.BlockSpec((1, C), lambda i: (0, 0))],
        out_specs=pl.BlockSpec((tm, C), lambda i: (i, 0)),
        out_shape=jax.ShapeDtypeStruct((T, C), x.dtype),
        compiler_params=_params("parallel"),
        name="scale_norm",
    )(x, ssq, g.reshape(1, C).astype(F32))


def _swiglu_in_kernel(x_ref, ssq_ref, gb_ref, wg_ref, wu_ref, wo_ref, o_ref, wob_ref):
    K, tn = wg_ref.shape
    gb = jnp.concatenate([gb_ref[...]] * (tn // LANES), axis=1)
    wg = (wg_ref[...] * gb).astype(BF16)
    wu = (wu_ref[...] * gb).astype(BF16)
    rc = min(SWIGLU_ROW_CHUNK, x_ref.shape[0])
    for r in range(x_ref.shape[0] // rc):
        rows = slice(r * rc, (r + 1) * rc)
        x = x_ref[rows, :]
        rs = _row_scale(ssq_ref[rows, :], K)
        g = jnp.dot(x, wg, preferred_element_type=F32) * rs
        u = jnp.dot(x, wu, preferred_element_type=F32) * rs
        o_ref[rows, :] = (g * jax.nn.sigmoid(g) * u).astype(o_ref.dtype)
    wob_ref[...] = wo_ref[...].astype(wob_ref.dtype)


def swiglu_in(hb, ssq, gain, w_in, w_out, layer, tm=2048, tn=256):
    T, K = hb.shape
    tm = min(tm, T)
    F = w_in.shape[2] // 2
    N = w_out.shape[2]
    nf = F // tn
    steps = (T // tm) * nf
    slab = F // steps
    assert F % steps == 0 and slab % 16 == 0, (F, steps)
    gb = jnp.broadcast_to(gain.astype(F32)[:, None], (K, LANES))
    return pl.pallas_call(
        _swiglu_in_kernel,
        grid=(T // tm, nf),
        in_specs=[pl.BlockSpec((tm, K), lambda i, j: (i, 0)),
                  pl.BlockSpec((tm, LANES), lambda i, j: (i, 0)),
                  pl.BlockSpec((K, LANES), lambda i, j: (0, 0)),
                  pl.BlockSpec((None, K, tn), lambda i, j: (layer, 0, j)),
                  pl.BlockSpec((None, K, tn), lambda i, j: (layer, 0, j + nf)),
                  pl.BlockSpec((None, slab, N), lambda i, j: (layer, i * nf + j, 0))],
        out_specs=[pl.BlockSpec((tm, tn), lambda i, j: (i, j)),
                   pl.BlockSpec((slab, N), lambda i, j: (i * nf + j, 0))],
        out_shape=[jax.ShapeDtypeStruct((T, F), BF16), jax.ShapeDtypeStruct((F, N), BF16)],
        compiler_params=_params("parallel", "arbitrary", vmem_limit=VMEM_LIMIT_WIDE_BYTES),
        name="swiglu_in",
    )(hb, ssq, gb, w_in, w_in, w_out)


def _emit_stream(h, j, h_ref, hb_ref, ssq_ref):
    h_ref[...] = h
    hb_ref[...] = h.astype(hb_ref.dtype)
    part = _lane_partial_sumsq(h)

    @pl.when(j == 0)
    def _():
        ssq_ref[...] = part

    @pl.when(j != 0)
    def _():
        ssq_ref[...] += part


def _stream_out(T, N, tm, tn):
    specs = [pl.BlockSpec((tm, tn), lambda i, j: (i, j)),
             pl.BlockSpec((tm, tn), lambda i, j: (i, j)),
             pl.BlockSpec((tm, LANES), lambda i, j: (i, 0))]
    shapes = [jax.ShapeDtypeStruct((T, N), F32), jax.ShapeDtypeStruct((T, N), BF16),
              jax.ShapeDtypeStruct((T, LANES), F32)]
    return specs, shapes


def _ffn_out_kernel(a_ref, w_ref, r_ref, h_ref, hb_ref, ssq_ref, *, scale):
    h = r_ref[...] + scale * jnp.dot(a_ref[...], w_ref[...], preferred_element_type=F32)
    h_ref[...] = h
    hb_ref[...] = h.astype(hb_ref.dtype)
    ssq_ref[...] = _lane_partial_sumsq(h)


def ffn_out(a, w, resid, scale, tm=512, tn=1024):
    T, K = a.shape
    tm = min(tm, T)
    N = w.shape[1]
    tn = _tile(N, tn)
    h, hb, ssq = pl.pallas_call(
        functools.partial(_ffn_out_kernel, scale=scale),
        grid=(N // tn, T // tm),
        in_specs=[pl.BlockSpec((tm, K), lambda j, i: (i, 0)),
                  pl.BlockSpec((K, tn), lambda j, i: (0, j), pipeline_mode=pl.Buffered(1)),
                  pl.BlockSpec((tm, tn), lambda j, i: (i, j))],
        out_specs=[pl.BlockSpec((tm, tn), lambda j, i: (i, j)),
                   pl.BlockSpec((tm, tn), lambda j, i: (i, j)),
                   pl.BlockSpec((None, tm, LANES), lambda j, i: (j, i, 0))],
        out_shape=[jax.ShapeDtypeStruct((T, N), F32), jax.ShapeDtypeStruct((T, N), BF16),
                   jax.ShapeDtypeStruct((N // tn, T, LANES), F32)],
        compiler_params=_params("parallel", "parallel", vmem_limit=VMEM_LIMIT_WIDE_BYTES),
        name="ffn_out",
    )(a, w, resid)
    return h, hb, jnp.sum(ssq, axis=0)


def _mix_out_kernel(ya_ref, yb_ref, yc_ref, gb_ref, gc_ref, w_ref, r_ref, h_ref, hb_ref, ssq_ref, yn_ref):
    j = pl.program_id(1)
    ca, cb = ya_ref.shape[1], yb_ref.shape[1]

    @pl.when(j == 0)
    def _():
        def norm(y_ref, g_ref):
            y = y_ref[...].astype(F32)
            ms = jnp.mean(y * y, axis=-1, keepdims=True)
            return (y * lax.rsqrt(ms + EPS) * g_ref[...]).astype(yn_ref.dtype)
        yn_ref[:, 0:ca] = ya_ref[...]
        yn_ref[:, ca:ca + cb] = norm(yb_ref, gb_ref)
        yn_ref[:, ca + cb:] = norm(yc_ref, gc_ref)

    acc = jnp.dot(yn_ref[...], w_ref[...], preferred_element_type=F32)
    _emit_stream(r_ref[...] + acc, j, h_ref, hb_ref, ssq_ref)


def mix_out(ya, yb, yc, gain_b, gain_c, w, layer, resid, tm=1024, tn=512):
    T, ca = ya.shape
    tm = min(tm, T)
    cb, cc = yb.shape[1], yc.shape[1]
    K, N = w.shape[1], w.shape[2]
    tn = _tile(N, tn)
    out_specs, out_shape = _stream_out(T, N, tm, tn)
    rows = lambda c: pl.BlockSpec((tm, c), lambda i, j: (i, 0))
    gain = lambda c: pl.BlockSpec((1, c), lambda i, j: (0, 0))
    return pl.pallas_call(
        _mix_out_kernel,
        grid=(T // tm, N // tn),
        in_specs=[rows(ca), rows(cb), rows(cc), gain(cb), gain(cc),
                  pl.BlockSpec((None, K, tn), lambda i, j: (layer, 0, j)),
                  pl.BlockSpec((tm, tn), lambda i, j: (i, j))],
        out_specs=out_specs,
        out_shape=out_shape,
        scratch_shapes=[pltpu.VMEM((tm, K), BF16)],
        compiler_params=_params("parallel", "arbitrary", vmem_limit=VMEM_LIMIT_WIDE_BYTES),
        name="mix_out",
    )(ya, yb, yc, gain_b.reshape(1, cb).astype(F32), gain_c.reshape(1, cc).astype(F32), w, resid)


def _mix_prep_kernel(wt_ref, g_ref, o_ref, *, q_lo, q_hi, q_scale, transpose):
    i = pl.program_id(1)
    scale = jnp.where((i >= q_lo) & (i < q_hi), q_scale, 1.0).astype(F32)
    w = wt_ref[...] * (g_ref[...] * scale)
    o_ref[...] = (w.T if transpose else w).astype(o_ref.dtype)


def mix_prep(w_t, gain, row0, n_rows, q_rows, q_scale, transpose=True, tr=256):
    L, d_in, K = w_t.shape
    tr = _tile(n_rows, tr)
    assert d_in % SUBLANES == 0 and row0 % SUBLANES == 0 and q_rows[0] % tr == 0 and q_rows[1] % tr == 0
    assert row0 + n_rows <= d_in
    rows = lambda l, i: (pl.multiple_of(l * d_in + row0 + i * tr, SUBLANES), 0)
    if transpose:
        out_spec, out_dims = pl.BlockSpec((None, K, tr), lambda l, i: (l, 0, i)), (L, K, n_rows)
    else:
        out_spec, out_dims = pl.BlockSpec((None, tr, K), lambda l, i: (l, i, 0)), (L, n_rows, K)
    return pl.pallas_call(
        functools.partial(_mix_prep_kernel, q_lo=q_rows[0] // tr, q_hi=q_rows[1] // tr, q_scale=q_scale,
                          transpose=transpose),
        grid=(L, n_rows // tr),
        in_specs=[pl.BlockSpec((pl.Element(tr), pl.Element(K)), rows),
                  pl.BlockSpec((None, 1, K), lambda l, i: (l, 0, 0))],
        out_specs=out_spec,
        out_shape=jax.ShapeDtypeStruct(out_dims, BF16),
        compiler_params=_params("parallel", "parallel"),
        name="mix_prep",
    )(w_t.reshape(L * d_in, K), gain.astype(F32)[:, None, :])


def _scaled_matmul_kernel(a_ref, ssq_ref, w_ref, o_ref):
    rs = _row_scale(ssq_ref[...], a_ref.shape[1])
    o_ref[...] = (jnp.dot(a_ref[...], w_ref[...], preferred_element_type=F32) * rs).astype(o_ref.dtype)


def scaled_matmul(hb, ssq, w, layer, col0, N, out_dtype, tm=1024, tn=1024):
    T, K = hb.shape
    tm = min(tm, T)
    tn = _tile(math.gcd(N, col0), tn)
    j0 = col0 // tn
    return pl.pallas_call(
        _scaled_matmul_kernel,
        grid=(T // tm, N // tn),
        in_specs=[pl.BlockSpec((tm, K), lambda i, j: (i, 0)),
                  pl.BlockSpec((tm, LANES), lambda i, j: (i, 0)),
                  pl.BlockSpec((None, K, tn), lambda i, j: (layer, 0, j0 + j))],
        out_specs=pl.BlockSpec((tm, tn), lambda i, j: (i, j)),
        out_shape=jax.ShapeDtypeStruct((T, N), out_dtype),
        compiler_params=_params("parallel", "arbitrary"),
        name="scaled_matmul",
    )(hb, ssq, w)


def _row_iota(shape):
    return lax.broadcasted_iota(jnp.int32, shape, 0)


def _log_sigmoid(x):
    return jnp.minimum(x, 0.0) - jnp.log1p(jnp.exp(-jnp.abs(x)))


def _softplus(x):
    return jnp.maximum(x, 0.0) + jnp.log1p(jnp.exp(-jnp.abs(x)))


def _forget_cumsum_kernel(x_ref, ssq_ref, w_ref, b_ref, o_ref, ot_ref, carry_ref):
    @pl.when(pl.program_id(1) == 0)
    def _():
        carry_ref[...] = jnp.zeros_like(carry_ref)

    rs = _row_scale(ssq_ref[...], x_ref.shape[1])
    z = _dot_nt(x_ref[...], w_ref[...]) * rs + b_ref[...]
    c = _log_sigmoid(z)
    ts = c.shape[0]
    rows = _row_iota(c.shape)
    shift = 1
    while shift < ts:
        c = c + jnp.where(rows >= shift, pltpu.roll(c, shift, 0), 0.0)
        shift *= 2
    c = c + carry_ref[...]
    o_ref[...] = c
    ot_ref[...] = c.T
    carry_ref[...] = c[ts - 1:ts, :]


def forget_cumsum(hb, ssq, w_ff, b_ff, layer, batch, ts=512):
    T, K = hb.shape
    S = T // batch
    ts = min(ts, S)
    nc = S // ts
    P = w_ff.shape[1]
    return pl.pallas_call(
        _forget_cumsum_kernel,
        grid=(batch, nc),
        in_specs=[pl.BlockSpec((ts, K), lambda b, c: (b * nc + c, 0)),
                  pl.BlockSpec((ts, LANES), lambda b, c: (b * nc + c, 0)),
                  pl.BlockSpec((None, P, K), lambda b, c: (layer, 0, 0)),
                  pl.BlockSpec((None, 1, P), lambda b, c: (layer, 0, 0))],
        out_specs=[pl.BlockSpec((ts, P), lambda b, c: (b * nc + c, 0)),
                   pl.BlockSpec((None, P, ts), lambda b, c: (b, 0, c))],
        out_shape=[jax.ShapeDtypeStruct((T, P), F32), jax.ShapeDtypeStruct((batch, P, S), F32)],
        scratch_shapes=[pltpu.VMEM((1, P), F32)],
        compiler_params=_params("parallel", "arbitrary"),
        name="forget_cumsum",
    )(hb, ssq, w_ff, b_ff)


def _lru_kernel(xa_ref, ga_ref, cw_ref, cb_ref, wa_ref, ba_ref, wx_ref, bx_ref, lam_ref, gn_ref,
                o_ref, tail_ref, h_ref):
    @pl.when(pl.program_id(1) == 0)
    def _():
        tail_ref[...] = jnp.zeros_like(tail_ref)
        h_ref[...] = jnp.zeros_like(h_ref)

    x = xa_ref[...]
    ts, C = x.shape
    xb = jnp.concatenate([tail_ref[...], x], axis=0)
    y = cb_ref[...] + cw_ref[CONV_WIDTH - 1:CONV_WIDTH, :] * x
    for back in range(1, CONV_WIDTH):
        y = y + cw_ref[CONV_WIDTH - 1 - back:CONV_WIDTH - back, :] * pltpu.roll(xb, back, 0)[SUBLANES:, :]
    tail_ref[...] = x[ts - SUBLANES:, :]

    rs, gs = [], []
    for g in range(C // LRU_BLOCK):
        yg = y[:, g * LRU_BLOCK:(g + 1) * LRU_BLOCK].astype(BF16)
        rs.append(jnp.dot(yg, wa_ref[g], preferred_element_type=F32))
        gs.append(jnp.dot(yg, wx_ref[g], preferred_element_type=F32))
    r = jax.nn.sigmoid(jnp.concatenate(rs, axis=1) + ba_ref[...])
    i = jax.nn.sigmoid(jnp.concatenate(gs, axis=1) + bx_ref[...])

    log_a = (-LRU_C) * r * _softplus(-lam_ref[...])
    a = jnp.exp(log_a)
    th = jnp.abs(jnp.tanh(log_a))
    u = jnp.sqrt(2.0 * th / (1.0 + th)) * (i * y)

    n_groups = ts // SUBLANES
    a = a.reshape(n_groups, SUBLANES, C)
    u = u.reshape(n_groups, SUBLANES, C)
    sub = lax.broadcasted_iota(jnp.int32, a.shape, 1)
    shift = 1
    while shift < SUBLANES:
        keep = sub >= shift
        a_prev = jnp.where(keep, pltpu.roll(a, shift, 1), 1.0)
        u_prev = jnp.where(keep, pltpu.roll(u, shift, 1), 0.0)
        u = a * u_prev + u
        a = a * a_prev
        shift *= 2
    carry = h_ref[...]
    groups = []
    for g in range(n_groups):
        hg = a[g] * carry + u[g]
        groups.append(hg)
        carry = hg[SUBLANES - 1:SUBLANES, :]
    h = jnp.concatenate(groups, axis=0)
    h_ref[...] = carry

    out = h * jax.nn.gelu(ga_ref[...])
    ms = jnp.mean(out * out, axis=-1, keepdims=True)
    o_ref[...] = (out * lax.rsqrt(ms + EPS) * gn_ref[...]).astype(o_ref.dtype)


def lru_branch(proj, conv_w, conv_b, w_a, b_a, w_x, b_x, lam, gain, batch, ts=256):
    T = proj.shape[0]
    C = proj.shape[1] // 2
    S = T // batch
    ts = min(ts, S)
    nc = S // ts
    row = lambda v: v.reshape(1, C).astype(F32)
    full2 = lambda shape: pl.BlockSpec(shape, lambda b, c: (0, 0))
    full3 = lambda shape: pl.BlockSpec(shape, lambda b, c: (0, 0, 0))
    return pl.pallas_call(
        _lru_kernel,
        grid=(batch, nc),
        in_specs=[pl.BlockSpec((ts, C), lambda b, c: (b * nc + c, 0)),
                  pl.BlockSpec((ts, C), lambda b, c: (b * nc + c, 1)),
                  full2((CONV_WIDTH, C)), full2((1, C)),
                  full3(w_a.shape), full2((1, C)),
                  full3(w_x.shape), full2((1, C)),
                  full2((1, C)), full2((1, C))],
        out_specs=pl.BlockSpec((ts, C), lambda b, c: (b * nc + c, 0)),
        out_shape=jax.ShapeDtypeStruct((T, C), BF16),
        scratch_shapes=[pltpu.VMEM((SUBLANES, C), F32), pltpu.VMEM((1, C), F32)],
        compiler_params=_params("parallel", "arbitrary"),
        name="lru_branch",
    )(proj, proj, conv_w.astype(F32), row(conv_b), w_a.astype(BF16), row(b_a),
      w_x.astype(BF16), row(b_x), row(lam), row(gain))


def _softmax2(z_chunks, row_offset, p_dtype):
    m = z_chunks[0]
    for zc in z_chunks[1:]:
        m = jnp.maximum(m, zc)
    m = jnp.max(m, axis=-1, keepdims=True)
    shift = m if row_offset is None else (m + row_offset) - row_offset
    ps = [jnp.exp2(zc - shift) for zc in z_chunks]
    l = ps[0]
    for p in ps[1:]:
        l = l + p
    l = jnp.sum(l, axis=-1, keepdims=True)
    p = jnp.concatenate(ps, axis=1) if len(ps) > 1 else ps[0]
    return p.astype(p_dtype), l


def _causal_sweep(heads, tq):
    nq = heads[0][0].shape[0] // tq
    items = [(i, head) for i in reversed(range(nq)) for head in heads]

    def scores(i, head):
        q_ref, k_ref = head[0], head[1]
        return _dot_nt(q_ref[i * tq:(i + 1) * tq, :], k_ref[0:(i + 1) * tq, :])

    def probs(i, head, s):
        v_ref, bias_chunks, row_offset = head[2], head[4], head[5]
        z = [bias_chunks(i, c, s[:, c * tq:(c + 1) * tq]) for c in range(i + 1)]
        return _softmax2(z, None if row_offset is None else row_offset(i), v_ref.dtype)

    def values(i, head, p, l):
        v_ref, o_ref = head[2], head[3]
        pv = jnp.dot(p, v_ref[0:(i + 1) * tq, :], preferred_element_type=F32)
        o_ref[i * tq:(i + 1) * tq, :] = (pv / l).astype(o_ref.dtype)

    s_next, pending = scores(*items[0]), None
    for n, item in enumerate(items):
        s = s_next
        if n + 1 < len(items):
            s_next = scores(*items[n + 1])
        p, l = probs(*item, s)
        if pending is not None:
            values(*pending)
        pending = item + (p, l)
    values(*pending)


def _fox_kernel(q_ref, k_ref, v_ref, cum_ref, cumt_ref, tri_ref, o_ref, *, tq):
    n_group = q_ref.shape[1] // HEAD_DIM
    lane = lax.broadcasted_iota(jnp.int32, (tq, cum_ref.shape[1]), 1)
    heads = []
    for h in range(n_group):
        cols = pl.ds(h * HEAD_DIM, HEAD_DIM)
        head = pl.program_id(1) * n_group + h
        ck2 = cumt_ref[pl.ds(head, 1), :] * LOG2E

        def bias_chunks(i, c, s, ck2=ck2):
            z = s - ck2[:, c * tq:(c + 1) * tq]
            return z + tri_ref[...] if c == i else z

        def row_offset(i, head=head):
            cum_t = jnp.where(lane == head, cum_ref[i * tq:(i + 1) * tq, :], 0.0)
            return jnp.sum(cum_t, axis=-1, keepdims=True) * LOG2E

        heads.append((q_ref.at[:, cols], k_ref.at[:, cols], v_ref.at[:, cols], o_ref.at[:, cols],
                      bias_chunks, row_offset))
    _causal_sweep(heads, tq)


def fox_attention(qkv, col0, n_heads, cum, cum_t, batch, tq=256):
    T = qkv.shape[0]
    S = T // batch
    tq = min(tq, S)
    hp = math.gcd(HEADS_PER_STEP, n_heads)
    w = hp * HEAD_DIM
    tri = np.where(np.arange(tq)[:, None] >= np.arange(tq)[None, :], 0.0, NEG_BIG).astype(np.float32)
    assert col0 % hp == 0
    qkv_spec = lambda off: pl.BlockSpec((S, w), lambda b, g: (b, (col0 + off) // hp + g))
    return pl.pallas_call(
        functools.partial(_fox_kernel, tq=tq),
        grid=(batch, n_heads // hp),
        in_specs=[qkv_spec(0), qkv_spec(n_heads), qkv_spec(2 * n_heads),
                  pl.BlockSpec((S, cum.shape[1]), lambda b, g: (b, 0)),
                  pl.BlockSpec((None, cum_t.shape[1], S), lambda b, g: (b, 0, 0)),
                  pl.BlockSpec((tq, tq), lambda b, g: (0, 0))],
        out_specs=pl.BlockSpec((S, w), lambda b, g: (b, g)),
        out_shape=jax.ShapeDtypeStruct((T, n_heads * HEAD_DIM), BF16),
        compiler_params=_params("parallel", "parallel"),
        name="fox_attention",
    )(qkv, qkv, qkv, cum, cum_t, jnp.asarray(tri))


def _dilated_kernel(q_ref, k_ref, v_ref, rc_ref, ra_ref, rb_ref, lm_ref, o_ref, qs_ref, ks_ref, *, tq):
    def rope(x):
        x = x.astype(F32)
        half = ROPE_DIM // 2
        return (x * rc_ref[...] + pltpu.roll(x, HEAD_DIM - half, 1) * ra_ref[...]
                + pltpu.roll(x, half, 1) * rb_ref[...])

    heads = [pl.ds(h * HEAD_DIM, HEAD_DIM) for h in range(q_ref.shape[1] // HEAD_DIM)]
    for cols in heads:
        qs_ref[:, cols] = rope(q_ref[:, cols]).astype(qs_ref.dtype)
        ks_ref[:, cols] = rope(k_ref[:, cols]).astype(ks_ref.dtype)
    add_log_multiplicity = lambda i, c, s: s + lm_ref[i - c]
    _causal_sweep([(qs_ref.at[:, cols], ks_ref.at[:, cols], v_ref.at[:, cols], o_ref.at[:, cols],
                    add_log_multiplicity, None) for cols in heads], tq)


def _dilated_log2_multiplicity(S, tq):
    nq = S // tq
    delta = (np.arange(nq)[:, None, None] * tq + np.arange(tq)[None, :, None]
             - np.arange(tq)[None, None, :])
    mult = np.zeros(delta.shape, np.float64)
    for window, dilation in DILATED_PATTERNS:
        mult += (delta >= 0) & (delta <= window) & (delta % dilation == 0)
    with np.errstate(divide="ignore"):
        return np.where(mult > 0, np.log2(mult), NEG_BIG).astype(np.float32)


def _rope_tables(S):
    inv = 1.0 / (ROPE_THETA ** (jnp.arange(0, ROPE_DIM, 2, dtype=F32) / ROPE_DIM))
    ang = jnp.arange(S, dtype=F32)[:, None] * inv[None, :]
    cos, sin = jnp.cos(ang), jnp.sin(ang)
    half = ROPE_DIM // 2
    pad = lambda t, lo: jnp.concatenate(
        [jnp.zeros((S, lo), F32), t, jnp.zeros((S, HEAD_DIM - lo - t.shape[1]), F32)], axis=1)
    rc = jnp.concatenate([cos, cos, jnp.ones((S, HEAD_DIM - ROPE_DIM), F32)], axis=1)
    ra = pad(-sin, 0)
    rb = pad(sin, half)
    return rc, ra, rb


def dilated_attention(qkv, col0, n_heads, batch, tq=256):
    T = qkv.shape[0]
    S = T // batch
    tq = min(tq, S)
    hd = HEAD_DIM
    nq = S // tq
    hp = math.gcd(HEADS_PER_STEP, n_heads)
    w = hp * hd
    assert col0 % hp == 0
    rc, ra, rb = _rope_tables(S)
    lm = jnp.asarray(_dilated_log2_multiplicity(S, tq))
    qkv_spec = lambda off: pl.BlockSpec((S, w), lambda b, g: (b, (col0 + off) // hp + g))
    tab = pl.BlockSpec((S, hd), lambda b, g: (0, 0))
    return pl.pallas_call(
        functools.partial(_dilated_kernel, tq=tq),
        grid=(batch, n_heads // hp),
        in_specs=[qkv_spec(0), qkv_spec(n_heads), qkv_spec(2 * n_heads), tab, tab, tab,
                  pl.BlockSpec((nq, tq, tq), lambda b, g: (0, 0, 0))],
        out_specs=pl.BlockSpec((S, w), lambda b, g: (b, g)),
        out_shape=jax.ShapeDtypeStruct((T, n_heads * hd), BF16),
        scratch_shapes=[pltpu.VMEM((S, w), BF16), pltpu.VMEM((S, w), BF16)],
        compiler_params=_params("parallel", "parallel"),
        name="dilated_attention",
    )(qkv, qkv, qkv, rc, ra, rb, lm)


def kernel(x, ffn1_norm, ffn1_w_in, ffn1_w_out, mix_norm, mix_w_in, conv_w, conv_b, lru_w_a, lru_b_a,
           lru_w_x, lru_b_x, lru_lam, fox_b_f, out_norm_lru, out_norm_fox, out_norm_dil, mix_w_out,
           ffn2_norm, ffn2_w_in, ffn2_w_out, final_norm):
    B, S, D = x.shape
    T = B * S
    depth = ffn1_norm.shape[0]
    d_lru = conv_w.shape[2]
    n_fox = fox_b_f.shape[1]
    d_fox = n_fox * HEAD_DIM
    d_dil = out_norm_dil.shape[1]
    n_dil = d_dil // HEAD_DIM

    mix_w_out_b = mix_w_out.astype(BF16)
    w_t = jnp.swapaxes(mix_w_in, 1, 2)
    q_scale = HEAD_DIM ** -0.5 * LOG2E
    ff0 = 2 * d_lru + 3 * d_fox
    w_head = mix_prep(w_t, mix_norm, 0, ff0, (2 * d_lru, 2 * d_lru + d_fox), q_scale)
    w_dil = mix_prep(w_t, mix_norm, ff0 + n_fox, 3 * d_dil, (0, d_dil), q_scale)
    w_ff = mix_prep(w_t, mix_norm, ff0, LANES, (0, 0), q_scale, transpose=False)
    b_ff = jnp.pad(fox_b_f.astype(F32), ((0, 0), (0, LANES - n_fox))).reshape(depth, 1, LANES)

    h = x.reshape(T, D)
    hb, ssq = rowstat(h)
    for l in range(depth):
        act, w_out_b = swiglu_in(hb, ssq, ffn1_norm[l], ffn1_w_in, ffn1_w_out, l)
        h, hb, ssq = ffn_out(act, w_out_b, h, 0.5)

        proj_lru = scaled_matmul(hb, ssq, w_head, l, 0, 2 * d_lru, F32)
        proj_fox = scaled_matmul(hb, ssq, w_head, l, 2 * d_lru, 3 * d_fox, BF16)
        proj_dil = scaled_matmul(hb, ssq, w_dil, l, 0, 3 * d_dil, BF16)
        cum, cum_t = forget_cumsum(hb, ssq, w_ff, b_ff, l, B)
        y_a = lru_branch(proj_lru, conv_w[l], conv_b[l], lru_w_a[l], lru_b_a[l], lru_w_x[l], lru_b_x[l],
                         lru_lam[l], out_norm_lru[l], B)
        y_b = fox_attention(proj_fox, 0, n_fox, cum, cum_t, B)
        y_c = dilated_attention(proj_dil, 0, n_dil, B)
        h, hb, ssq = mix_out(y_a, y_b, y_c, out_norm_fox[l], out_norm_dil[l], mix_w_out_b, l, h)

        act, w_out_b = swiglu_in(hb, ssq, ffn2_norm[l], ffn2_w_in, ffn2_w_out, l)
        h, hb, ssq = ffn_out(act, w_out_b, h, 0.5)
    return scale_norm(h, ssq, final_norm).reshape(B, S, D)
```

```python
import functools
import math

import numpy as np
import jax
import jax.numpy as jnp
from jax import lax
from jax.experimental import pallas as pl
from jax.experimental.pallas import tpu as pltpu

F32 = jnp.float32
BF16 = jnp.bfloat16

LANES = 128
SUBLANES = 8
HEAD_DIM = 128
LRU_BLOCK = 128
CONV_WIDTH = 4
LRU_C = 8.0
ROPE_THETA = 500000.0
ROPE_DIM = HEAD_DIM // 4
DILATED_PATTERNS = ((128, 1), (512, 4), (2048, 16))
EPS = 1e-6
NEG_BIG = -1e30
LOG2E = 1.4426950408889634
SWIGLU_ROW_CHUNK = 256
HEADS_PER_STEP = 2

VMEM_BYTES_V7X = 64 * 1024 * 1024
VMEM_LIMIT_BYTES = VMEM_BYTES_V7X - 8 * 1024 * 1024
VMEM_LIMIT_WIDE_BYTES = VMEM_BYTES_V7X - 2 * 1024 * 1024


def _params(*sem, vmem_limit=None):
    return pltpu.CompilerParams(dimension_semantics=sem, vmem_limit_bytes=vmem_limit or VMEM_LIMIT_BYTES)


def _tile(n, preferred):
    t = min(preferred, n)
    while n % t:
        t //= 2
    assert t % LANES == 0 or t == n, (n, preferred)
    return t


def _lane_partial_sumsq(x):
    x2 = x * x
    acc = x2[:, 0:LANES]
    for k in range(1, x.shape[1] // LANES):
        acc = acc + x2[:, k * LANES:(k + 1) * LANES]
    return acc


def _dot_nt(a, b):
    return lax.dot_general(a, b, (((1,), (1,)), ((), ())), preferred_element_type=F32)


def _row_scale(ssq, width):
    total = jnp.sum(jnp.sum(ssq, axis=0), axis=-1, keepdims=True)
    return lax.rsqrt(total * (1.0 / width) + EPS)


def _ssq_spec(ssq, tm, index):
    return pl.BlockSpec((ssq.shape[0], tm, LANES), lambda *ids: (0, index(*ids), 0))


def _rowstat_kernel(x_ref, hb_ref, ssq_ref):
    x = x_ref[...]
    hb_ref[...] = x.astype(hb_ref.dtype)
    ssq_ref[0] = _lane_partial_sumsq(x)


def rowstat(x, tm=512):
    T, C = x.shape
    tm = min(tm, T)
    return pl.pallas_call(
        _rowstat_kernel,
        grid=(T // tm,),
        in_specs=[pl.BlockSpec((tm, C), lambda i: (i, 0))],
        out_specs=[pl.BlockSpec((tm, C), lambda i: (i, 0)), pl.BlockSpec((1, tm, LANES), lambda i: (0, i, 0))],
        out_shape=[jax.ShapeDtypeStruct((T, C), BF16), jax.ShapeDtypeStruct((1, T, LANES), F32)],
        compiler_params=_params("parallel"),
        name="rowstat",
    )(x)


def _scale_norm_kernel(x_ref, ssq_ref, g_ref, o_ref):
    x = x_ref[...]
    o_ref[...] = (x * _row_scale(ssq_ref[...], x.shape[1]) * g_ref[...]).astype(o_ref.dtype)


def scale_norm(x, ssq, g, tm=512):
    T, C = x.shape
    tm = min(tm, T)
    return pl.pallas_call(
        _scale_norm_kernel,
        grid=(T // tm,),
        in_specs=[pl.BlockSpec((tm, C), lambda i: (i, 0)),
                  _ssq_spec(ssq, tm, lambda i: i),
                  pl.BlockSpec((1, C), lambda i: (0, 0))],
        out_specs=pl.BlockSpec((tm, C), lambda i: (i, 0)),
        out_shape=jax.ShapeDtypeStruct((T, C), x.dtype),
        compiler_params=_params("parallel"),
        name="scale_norm",
    )(x, ssq, g.reshape(1, C).astype(F32))


def _swiglu_in_kernel(x_ref, ssq_ref, gb_ref, wg_ref, wu_ref, wo_ref, o_ref, wob_ref):
    K, tn = wg_ref.shape
    gb = jnp.concatenate([gb_ref[...]] * (tn // LANES), axis=1)
    wg = (wg_ref[...] * gb).astype(BF16)
    wu = (wu_ref[...] * gb).astype(BF16)
    rc = min(SWIGLU_ROW_CHUNK, x_ref.shape[0])
    for r in range(x_ref.shape[0] // rc):
        rows = slice(r * rc, (r + 1) * rc)
        x = x_ref[rows, :]
        rs = _row_scale(ssq_ref[:, rows, :], K)
        g = jnp.dot(x, wg, preferred_element_type=F32) * rs
        u = jnp.dot(x, wu, preferred_element_type=F32) * rs
        o_ref[rows, :] = (g * jax.nn.sigmoid(g) * u).astype(o_ref.dtype)
    wob_ref[...] = wo_ref[...].astype(wob_ref.dtype)


def swiglu_in(hb, ssq, gain, w_in, w_out, layer, tm=2048, tn=256):
    T, K = hb.shape
    tm = min(tm, T)
    if ssq.shape[0] > 1:
        ssq = jnp.sum(ssq, axis=0, keepdims=True)
    F = w_in.shape[2] // 2
    N = w_out.shape[2]
    nf = F // tn
    steps = (T // tm) * nf
    slab = F // steps
    assert F % steps == 0 and slab % 16 == 0, (F, steps)
    gb = jnp.broadcast_to(gain.astype(F32)[:, None], (K, LANES))
    return pl.pallas_call(
        _swiglu_in_kernel,
        grid=(T // tm, nf),
        in_specs=[pl.BlockSpec((tm, K), lambda i, j: (i, 0)),
                  _ssq_spec(ssq, tm, lambda i, j: i),
                  pl.BlockSpec((K, LANES), lambda i, j: (0, 0)),
                  pl.BlockSpec((None, K, tn), lambda i, j: (layer, 0, j)),
                  pl.BlockSpec((None, K, tn), lambda i, j: (layer, 0, j + nf)),
                  pl.BlockSpec((None, slab, N), lambda i, j: (layer, i * nf + j, 0))],
        out_specs=[pl.BlockSpec((tm, tn), lambda i, j: (i, j)),
                   pl.BlockSpec((slab, N), lambda i, j: (i * nf + j, 0))],
        out_shape=[jax.ShapeDtypeStruct((T, F), BF16), jax.ShapeDtypeStruct((F, N), BF16)],
        compiler_params=_params("parallel", "arbitrary", vmem_limit=VMEM_LIMIT_WIDE_BYTES),
        name="swiglu_in",
    )(hb, ssq, gb, w_in, w_in, w_out)


def _emit_stream(h, j, h_ref, hb_ref, ssq_ref):
    h_ref[...] = h
    hb_ref[...] = h.astype(hb_ref.dtype)
    part = _lane_partial_sumsq(h)

    @pl.when(j == 0)
    def _():
        ssq_ref[0] = part

    @pl.when(j != 0)
    def _():
        ssq_ref[0] += part


def _stream_out(T, N, tm, tn):
    specs = [pl.BlockSpec((tm, tn), lambda i, j: (i, j)),
             pl.BlockSpec((tm, tn), lambda i, j: (i, j)),
             pl.BlockSpec((1, tm, LANES), lambda i, j: (0, i, 0))]
    shapes = [jax.ShapeDtypeStruct((T, N), F32), jax.ShapeDtypeStruct((T, N), BF16),
              jax.ShapeDtypeStruct((1, T, LANES), F32)]
    return specs, shapes


def _ffn_out_kernel(a_ref, w_ref, r_ref, h_ref, hb_ref, ssq_ref, *, scale):
    h = r_ref[...] + scale * jnp.dot(a_ref[...], w_ref[...], preferred_element_type=F32)
    h_ref[...] = h
    hb_ref[...] = h.astype(hb_ref.dtype)
    ssq_ref[...] = _lane_partial_sumsq(h)


def ffn_out(a, w, resid, scale, tm=512, tn=1024):
    T, K = a.shape
    tm = min(tm, T)
    N = w.shape[1]
    tn = _tile(N, tn)
    return pl.pallas_call(
        functools.partial(_ffn_out_kernel, scale=scale),
        grid=(N // tn, T // tm),
        in_specs=[pl.BlockSpec((tm, K), lambda j, i: (i, 0)),
                  pl.BlockSpec((K, tn), lambda j, i: (0, j), pipeline_mode=pl.Buffered(1)),
                  pl.BlockSpec((tm, tn), lambda j, i: (i, j))],
        out_specs=[pl.BlockSpec((tm, tn), lambda j, i: (i, j)),
                   pl.BlockSpec((tm, tn), lambda j, i: (i, j)),
                   pl.BlockSpec((None, tm, LANES), lambda j, i: (j, i, 0))],
        out_shape=[jax.ShapeDtypeStruct((T, N), F32), jax.ShapeDtypeStruct((T, N), BF16),
                   jax.ShapeDtypeStruct((N // tn, T, LANES), F32)],
        compiler_params=_params("parallel", "parallel", vmem_limit=VMEM_LIMIT_WIDE_BYTES),
        name="ffn_out",
    )(a, w, resid)


def _mix_out_kernel(ya_ref, yb_ref, yc_ref, gb_ref, gc_ref, w_ref, r_ref, h_ref, hb_ref, ssq_ref, yn_ref):
    j = pl.program_id(1)
    ca, cb = ya_ref.shape[1], yb_ref.shape[1]

    @pl.when(j == 0)
    def _():
        def norm(y_ref, g_ref):
            y = y_ref[...].astype(F32)
            ms = jnp.mean(y * y, axis=-1, keepdims=True)
            return (y * lax.rsqrt(ms + EPS) * g_ref[...]).astype(yn_ref.dtype)
        yn_ref[:, 0:ca] = ya_ref[...]
        yn_ref[:, ca:ca + cb] = norm(yb_ref, gb_ref)
        yn_ref[:, ca + cb:] = norm(yc_ref, gc_ref)

    acc = jnp.dot(yn_ref[...], w_ref[...], preferred_element_type=F32)
    _emit_stream(r_ref[...] + acc, j, h_ref, hb_ref, ssq_ref)


def mix_out(ya, yb, yc, gain_b, gain_c, w, resid, tm=1024, tn=512):
    T, ca = ya.shape
    tm = min(tm, T)
    cb, cc = yb.shape[1], yc.shape[1]
    K, N = w.shape
    tn = _tile(N, tn)
    out_specs, out_shape = _stream_out(T, N, tm, tn)
    rows = lambda c: pl.BlockSpec((tm, c), lambda i, j: (i, 0))
    gain = lambda c: pl.BlockSpec((1, c), lambda i, j: (0, 0))
    return pl.pallas_call(
        _mix_out_kernel,
        grid=(T // tm, N // tn),
        in_specs=[rows(ca), rows(cb), rows(cc), gain(cb), gain(cc),
                  pl.BlockSpec((K, tn), lambda i, j: (0, j)),
                  pl.BlockSpec((tm, tn), lambda i, j: (i, j))],
        out_specs=out_specs,
        out_shape=out_shape,
        scratch_shapes=[pltpu.VMEM((tm, K), BF16)],
        compiler_params=_params("parallel", "arbitrary", vmem_limit=VMEM_LIMIT_WIDE_BYTES),
        name="mix_out",
    )(ya, yb, yc, gain_b.reshape(1, cb).astype(F32), gain_c.reshape(1, cc).astype(F32), w, resid)


def _mix_prep_kernel(wt_ref, g_ref, o_ref, *, q_lo, q_hi, q_scale, transpose):
    i = pl.program_id(1)
    scale = jnp.where((i >= q_lo) & (i < q_hi), q_scale, 1.0).astype(F32)
    w = wt_ref[...] * (g_ref[...] * scale)
    o_ref[...] = (w.T if transpose else w).astype(o_ref.dtype)


def mix_prep(w_t, gain, row0, n_rows, q_rows, q_scale, transpose=True, tr=256):
    L, d_in, K = w_t.shape
    tr = _tile(n_rows, tr)
    assert d_in % SUBLANES == 0 and row0 % SUBLANES == 0 and q_rows[0] % tr == 0 and q_rows[1] % tr == 0
    assert row0 + n_rows <= d_in
    rows = lambda l, i: (pl.multiple_of(l * d_in + row0 + i * tr, SUBLANES), 0)
    if transpose:
        out_spec, out_dims = pl.BlockSpec((None, K, tr), lambda l, i: (l, 0, i)), (L, K, n_rows)
    else:
        out_spec, out_dims = pl.BlockSpec((None, tr, K), lambda l, i: (l, i, 0)), (L, n_rows, K)
    return pl.pallas_call(
        functools.partial(_mix_prep_kernel, q_lo=q_rows[0] // tr, q_hi=q_rows[1] // tr, q_scale=q_scale,
                          transpose=transpose),
        grid=(L, n_rows // tr),
        in_specs=[pl.BlockSpec((pl.Element(tr), pl.Element(K)), rows),
                  pl.BlockSpec((None, 1, K), lambda l, i: (l, 0, 0))],
        out_specs=out_spec,
        out_shape=jax.ShapeDtypeStruct(out_dims, BF16),
        compiler_params=_params("parallel", "parallel"),
        name="mix_prep",
    )(w_t.reshape(L * d_in, K), gain.astype(F32)[:, None, :])


def _scaled_matmul_kernel(a_ref, ssq_ref, w_ref, o_ref):
    rs = _row_scale(ssq_ref[...], a_ref.shape[1])
    o_ref[...] = (jnp.dot(a_ref[...], w_ref[...], preferred_element_type=F32) * rs).astype(o_ref.dtype)


def scaled_matmul(hb, ssq, w, layer, col0, N, out_dtype, tm=1024, tn=1024):
    T, K = hb.shape
    tm = min(tm, T)
    tn = _tile(math.gcd(N, col0), tn)
    j0 = col0 // tn
    return pl.pallas_call(
        _scaled_matmul_kernel,
        grid=(T // tm, N // tn),
        in_specs=[pl.BlockSpec((tm, K), lambda i, j: (i, 0)),
                  _ssq_spec(ssq, tm, lambda i, j: i),
                  pl.BlockSpec((None, K, tn), lambda i, j: (layer, 0, j0 + j))],
        out_specs=pl.BlockSpec((tm, tn), lambda i, j: (i, j)),
        out_shape=jax.ShapeDtypeStruct((T, N), out_dtype),
        compiler_params=_params("parallel", "arbitrary"),
        name="scaled_matmul",
    )(hb, ssq, w)


def _row_iota(shape):
    return lax.broadcasted_iota(jnp.int32, shape, 0)


def _log_sigmoid(x):
    return jnp.minimum(x, 0.0) - jnp.log1p(jnp.exp(-jnp.abs(x)))


def _softplus(x):
    return jnp.maximum(x, 0.0) + jnp.log1p(jnp.exp(-jnp.abs(x)))


def _forget_cumsum_kernel(x_ref, ssq_ref, w_ref, b_ref, o_ref, ot_ref, carry_ref):
    @pl.when(pl.program_id(1) == 0)
    def _():
        carry_ref[...] = jnp.zeros_like(carry_ref)

    rs = _row_scale(ssq_ref[...], x_ref.shape[1])
    z = _dot_nt(x_ref[...], w_ref[...]) * rs + b_ref[...]
    c = _log_sigmoid(z)
    ts = c.shape[0]
    rows = _row_iota(c.shape)
    shift = 1
    while shift < ts:
        c = c + jnp.where(rows >= shift, pltpu.roll(c, shift, 0), 0.0)
        shift *= 2
    c = c + carry_ref[...]
    o_ref[...] = c
    ot_ref[...] = c.T
    carry_ref[...] = c[ts - 1:ts, :]


def forget_cumsum(hb, ssq, w_ff, b_ff, layer, batch, ts=512):
    T, K = hb.shape
    S = T // batch
    ts = min(ts, S)
    nc = S // ts
    P = w_ff.shape[1]
    return pl.pallas_call(
        _forget_cumsum_kernel,
        grid=(batch, nc),
        in_specs=[pl.BlockSpec((ts, K), lambda b, c: (b * nc + c, 0)),
                  _ssq_spec(ssq, ts, lambda b, c: b * nc + c),
                  pl.BlockSpec((None, P, K), lambda b, c: (layer, 0, 0)),
                  pl.BlockSpec((None, 1, P), lambda b, c: (layer, 0, 0))],
        out_specs=[pl.BlockSpec((ts, P), lambda b, c: (b * nc + c, 0)),
                   pl.BlockSpec((None, P, ts), lambda b, c: (b, 0, c))],
        out_shape=[jax.ShapeDtypeStruct((T, P), F32), jax.ShapeDtypeStruct((batch, P, S), F32)],
        scratch_shapes=[pltpu.VMEM((1, P), F32)],
        compiler_params=_params("parallel", "arbitrary"),
        name="forget_cumsum",
    )(hb, ssq, w_ff, b_ff)


def _lru_kernel(xa_ref, ga_ref, cw_ref, cb_ref, wa_ref, ba_ref, wx_ref, bx_ref, lam_ref, gn_ref,
                o_ref, tail_ref, h_ref):
    @pl.when(pl.program_id(1) == 0)
    def _():
        tail_ref[...] = jnp.zeros_like(tail_ref)
        h_ref[...] = jnp.zeros_like(h_ref)

    x = xa_ref[...]
    ts, C = x.shape
    xb = jnp.concatenate([tail_ref[...], x], axis=0)
    y = cb_ref[...] + cw_ref[CONV_WIDTH - 1:CONV_WIDTH, :] * x
    for back in range(1, CONV_WIDTH):
        y = y + cw_ref[CONV_WIDTH - 1 - back:CONV_WIDTH - back, :] * pltpu.roll(xb, back, 0)[SUBLANES:, :]
    tail_ref[...] = x[ts - SUBLANES:, :]

    rs, gs = [], []
    for g in range(C // LRU_BLOCK):
        yg = y[:, g * LRU_BLOCK:(g + 1) * LRU_BLOCK].astype(BF16)
        rs.append(jnp.dot(yg, wa_ref[g], preferred_element_type=F32))
        gs.append(jnp.dot(yg, wx_ref[g], preferred_element_type=F32))
    r = jax.nn.sigmoid(jnp.concatenate(rs, axis=1) + ba_ref[...])
    i = jax.nn.sigmoid(jnp.concatenate(gs, axis=1) + bx_ref[...])

    log_a = (-LRU_C) * r * _softplus(-lam_ref[...])
    a = jnp.exp(log_a)
    th = jnp.abs(jnp.tanh(log_a))
    u = jnp.sqrt(2.0 * th / (1.0 + th)) * (i * y)

    n_groups = ts // SUBLANES
    a = a.reshape(n_groups, SUBLANES, C)
    u = u.reshape(n_groups, SUBLANES, C)
    sub = lax.broadcasted_iota(jnp.int32, a.shape, 1)
    shift = 1
    while shift < SUBLANES:
        keep = sub >= shift
        a_prev = jnp.where(keep, pltpu.roll(a, shift, 1), 1.0)
        u_prev = jnp.where(keep, pltpu.roll(u, shift, 1), 0.0)
        u = a * u_prev + u
        a = a * a_prev
        shift *= 2
    carry = h_ref[...]
    groups = []
    for g in range(n_groups):
        hg = a[g] * carry + u[g]
        groups.append(hg)
        carry = hg[SUBLANES - 1:SUBLANES, :]
    h = jnp.concatenate(groups, axis=0)
    h_ref[...] = carry

    out = h * jax.nn.gelu(ga_ref[...])
    ms = jnp.mean(out * out, axis=-1, keepdims=True)
    o_ref[...] = (out * lax.rsqrt(ms + EPS) * gn_ref[...]).astype(o_ref.dtype)


def lru_branch(proj, conv_w, conv_b, w_a, b_a, w_x, b_x, lam, gain, batch, ts=256):
    T = proj.shape[0]
    C = proj.shape[1] // 2
    S = T // batch
    ts = min(ts, S)
    nc = S // ts
    row = lambda v: v.reshape(1, C).astype(F32)
    full2 = lambda shape: pl.BlockSpec(shape, lambda b, c: (0, 0))
    full3 = lambda shape: pl.BlockSpec(shape, lambda b, c: (0, 0, 0))
    return pl.pallas_call(
        _lru_kernel,
        grid=(batch, nc),
        in_specs=[pl.BlockSpec((ts, C), lambda b, c: (b * nc + c, 0)),
                  pl.BlockSpec((ts, C), lambda b, c: (b * nc + c, 1)),
                  full2((CONV_WIDTH, C)), full2((1, C)),
                  full3(w_a.shape), full2((1, C)),
                  full3(w_x.shape), full2((1, C)),
                  full2((1, C)), full2((1, C))],
        out_specs=pl.BlockSpec((ts, C), lambda b, c: (b * nc + c, 0)),
        out_shape=jax.ShapeDtypeStruct((T, C), BF16),
        scratch_shapes=[pltpu.VMEM((SUBLANES, C), F32), pltpu.VMEM((1, C), F32)],
        compiler_params=_params("parallel", "arbitrary"),
        name="lru_branch",
    )(proj, proj, conv_w.astype(F32), row(conv_b), w_a.astype(BF16), row(b_a),
      w_x.astype(BF16), row(b_x), row(lam), row(gain))


def _softmax2(z_chunks, row_offset, p_dtype):
    m = z_chunks[0]
    for zc in z_chunks[1:]:
        m = jnp.maximum(m, zc)
    m = jnp.max(m, axis=-1, keepdims=True)
    shift = m if row_offset is None else (m + row_offset) - row_offset
    ps = [jnp.exp2(zc - shift) for zc in z_chunks]
    l = ps[0]
    for p in ps[1:]:
        l = l + p
    l = jnp.sum(l, axis=-1, keepdims=True)
    p = jnp.concatenate(ps, axis=1) if len(ps) > 1 else ps[0]
    return p.astype(p_dtype), l


def _causal_sweep(heads, tq):
    nq = heads[0][0].shape[0] // tq
    items = [(i, head) for i in reversed(range(nq)) for head in heads]

    def scores(i, head):
        q_ref, k_ref = head[0], head[1]
        return _dot_nt(q_ref[i * tq:(i + 1) * tq, :], k_ref[0:(i + 1) * tq, :])

    def probs(i, head, s):
        v_ref, bias_chunks, row_offset = head[2], head[4], head[5]
        z = [bias_chunks(i, c, s[:, c * tq:(c + 1) * tq]) for c in range(i + 1)]
        return _softmax2(z, None if row_offset is None else row_offset(i), v_ref.dtype)

    def values(i, head, p, l):
        v_ref, o_ref = head[2], head[3]
        pv = jnp.dot(p, v_ref[0:(i + 1) * tq, :], preferred_element_type=F32)
        o_ref[i * tq:(i + 1) * tq, :] = (pv / l).astype(o_ref.dtype)

    s_next, pending = scores(*items[0]), None
    for n, item in enumerate(items):
        s = s_next
        if n + 1 < len(items):
            s_next = scores(*items[n + 1])
        p, l = probs(*item, s)
        if pending is not None:
            values(*pending)
        pending = item + (p, l)
    values(*pending)


def _fox_kernel(q_ref, k_ref, v_ref, cum_ref, cumt_ref, tri_ref, o_ref, *, tq):
    n_group = q_ref.shape[1] // HEAD_DIM
    lane = lax.broadcasted_iota(jnp.int32, (tq, cum_ref.shape[1]), 1)
    heads = []
    for h in range(n_group):
        cols = pl.ds(h * HEAD_DIM, HEAD_DIM)
        head = pl.program_id(1) * n_group + h
        ck2 = cumt_ref[pl.ds(head, 1), :] * LOG2E

        def bias_chunks(i, c, s, ck2=ck2):
            z = s - ck2[:, c * tq:(c + 1) * tq]
            return z + tri_ref[...] if c == i else z

        def row_offset(i, head=head):
            cum_t = jnp.where(lane == head, cum_ref[i * tq:(i + 1) * tq, :], 0.0)
            return jnp.sum(cum_t, axis=-1, keepdims=True) * LOG2E

        heads.append((q_ref.at[:, cols], k_ref.at[:, cols], v_ref.at[:, cols], o_ref.at[:, cols],
                      bias_chunks, row_offset))
    _causal_sweep(heads, tq)


def fox_attention(qkv, col0, n_heads, cum, cum_t, batch, tq=256):
    T = qkv.shape[0]
    S = T // batch
    tq = min(tq, S)
    hp = math.gcd(HEADS_PER_STEP, n_heads)
    w = hp * HEAD_DIM
    tri = np.where(np.arange(tq)[:, None] >= np.arange(tq)[None, :], 0.0, NEG_BIG).astype(np.float32)
    assert col0 % hp == 0
    qkv_spec = lambda off: pl.BlockSpec((S, w), lambda b, g: (b, (col0 + off) // hp + g))
    return pl.pallas_call(
        functools.partial(_fox_kernel, tq=tq),
        grid=(batch, n_heads // hp),
        in_specs=[qkv_spec(0), qkv_spec(n_heads), qkv_spec(2 * n_heads),
                  pl.BlockSpec((S, cum.shape[1]), lambda b, g: (b, 0)),
                  pl.BlockSpec((None, cum_t.shape[1], S), lambda b, g: (b, 0, 0)),
                  pl.BlockSpec((tq, tq), lambda b, g: (0, 0))],
        out_specs=pl.BlockSpec((S, w), lambda b, g: (b, g)),
        out_shape=jax.ShapeDtypeStruct((T, n_heads * HEAD_DIM), BF16),
        compiler_params=_params("parallel", "parallel"),
        name="fox_attention",
    )(qkv, qkv, qkv, cum, cum_t, jnp.asarray(tri))


def _dilated_kernel(q_ref, k_ref, v_ref, rc_ref, ra_ref, rb_ref, lm_ref, wo_ref, o_ref, wob_ref, qs_ref, ks_ref,
                    *, tq):
    def rope(x):
        x = x.astype(F32)
        half = ROPE_DIM // 2
        return (x * rc_ref[...] + pltpu.roll(x, HEAD_DIM - half, 1) * ra_ref[...]
                + pltpu.roll(x, half, 1) * rb_ref[...])

    heads = [pl.ds(h * HEAD_DIM, HEAD_DIM) for h in range(q_ref.shape[1] // HEAD_DIM)]
    for cols in heads:
        qs_ref[:, cols] = rope(q_ref[:, cols]).astype(qs_ref.dtype)
        ks_ref[:, cols] = rope(k_ref[:, cols]).astype(ks_ref.dtype)
    add_log_multiplicity = lambda i, c, s: s + lm_ref[i - c]
    _causal_sweep([(qs_ref.at[:, cols], ks_ref.at[:, cols], v_ref.at[:, cols], o_ref.at[:, cols],
                    add_log_multiplicity, None) for cols in heads], tq)
    wob_ref[...] = wo_ref[...].astype(wob_ref.dtype)


def _dilated_log2_multiplicity(S, tq):
    nq = S // tq
    delta = (np.arange(nq)[:, None, None] * tq + np.arange(tq)[None, :, None]
             - np.arange(tq)[None, None, :])
    mult = np.zeros(delta.shape, np.float64)
    for window, dilation in DILATED_PATTERNS:
        mult += (delta >= 0) & (delta <= window) & (delta % dilation == 0)
    with np.errstate(divide="ignore"):
        return np.where(mult > 0, np.log2(mult), NEG_BIG).astype(np.float32)


def _rope_tables(S):
    inv = 1.0 / (ROPE_THETA ** (jnp.arange(0, ROPE_DIM, 2, dtype=F32) / ROPE_DIM))
    ang = jnp.arange(S, dtype=F32)[:, None] * inv[None, :]
    cos, sin = jnp.cos(ang), jnp.sin(ang)
    half = ROPE_DIM // 2
    pad = lambda t, lo: jnp.concatenate(
        [jnp.zeros((S, lo), F32), t, jnp.zeros((S, HEAD_DIM - lo - t.shape[1]), F32)], axis=1)
    rc = jnp.concatenate([cos, cos, jnp.ones((S, HEAD_DIM - ROPE_DIM), F32)], axis=1)
    ra = pad(-sin, 0)
    rb = pad(sin, half)
    return rc, ra, rb


def dilated_attention(qkv, col0, n_heads, batch, w_out, layer, tq=256):
    T = qkv.shape[0]
    S = T // batch
    tq = min(tq, S)
    hd = HEAD_DIM
    nq = S // tq
    hp = math.gcd(HEADS_PER_STEP, n_heads)
    w = hp * hd
    assert col0 % hp == 0
    ng = n_heads // hp
    Kw, Nw = w_out.shape[1], w_out.shape[2]
    slab = Kw // (batch * ng)
    assert Kw % (batch * ng) == 0 and slab % 16 == 0, (Kw, batch, ng)
    rc, ra, rb = _rope_tables(S)
    lm = jnp.asarray(_dilated_log2_multiplicity(S, tq))
    qkv_spec = lambda off: pl.BlockSpec((S, w), lambda b, g: (b, (col0 + off) // hp + g))
    tab = pl.BlockSpec((S, hd), lambda b, g: (0, 0))
    return pl.pallas_call(
        functools.partial(_dilated_kernel, tq=tq),
        grid=(batch, n_heads // hp),
        in_specs=[qkv_spec(0), qkv_spec(n_heads), qkv_spec(2 * n_heads), tab, tab, tab,
                  pl.BlockSpec((nq, tq, tq), lambda b, g: (0, 0, 0)),
                  pl.BlockSpec((None, slab, Nw), lambda b, g: (layer, b * ng + g, 0))],
        out_specs=[pl.BlockSpec((S, w), lambda b, g: (b, g)),
                   pl.BlockSpec((slab, Nw), lambda b, g: (b * ng + g, 0))],
        out_shape=[jax.ShapeDtypeStruct((T, n_heads * hd), BF16), jax.ShapeDtypeStruct((Kw, Nw), BF16)],
        scratch_shapes=[pltpu.VMEM((S, w), BF16), pltpu.VMEM((S, w), BF16)],
        compiler_params=_params("parallel", "parallel"),
        name="dilated_attention",
    )(qkv, qkv, qkv, rc, ra, rb, lm, w_out)


def kernel(x, ffn1_norm, ffn1_w_in, ffn1_w_out, mix_norm, mix_w_in, conv_w, conv_b, lru_w_a, lru_b_a,
           lru_w_x, lru_b_x, lru_lam, fox_b_f, out_norm_lru, out_norm_fox, out_norm_dil, mix_w_out,
           ffn2_norm, ffn2_w_in, ffn2_w_out, final_norm):
    B, S, D = x.shape
    T = B * S
    depth = ffn1_norm.shape[0]
    d_lru = conv_w.shape[2]
    n_fox = fox_b_f.shape[1]
    d_fox = n_fox * HEAD_DIM
    d_dil = out_norm_dil.shape[1]
    n_dil = d_dil // HEAD_DIM

    w_t = jnp.swapaxes(mix_w_in, 1, 2)
    q_scale = HEAD_DIM ** -0.5 * LOG2E
    ff0 = 2 * d_lru + 3 * d_fox
    w_head = mix_prep(w_t, mix_norm, 0, ff0, (2 * d_lru, 2 * d_lru + d_fox), q_scale)
    w_dil = mix_prep(w_t, mix_norm, ff0 + n_fox, 3 * d_dil, (0, d_dil), q_scale)
    w_ff = mix_prep(w_t, mix_norm, ff0, LANES, (0, 0), q_scale, transpose=False)
    b_ff = jnp.pad(fox_b_f.astype(F32), ((0, 0), (0, LANES - n_fox))).reshape(depth, 1, LANES)

    h = x.reshape(T, D)
    hb, ssq = rowstat(h)
    for l in range(depth):
        act, w_out_b = swiglu_in(hb, ssq, ffn1_norm[l], ffn1_w_in, ffn1_w_out, l)
        h, hb, ssq = ffn_out(act, w_out_b, h, 0.5)

        proj_lru = scaled_matmul(hb, ssq, w_head, l, 0, 2 * d_lru, F32)
        proj_fox = scaled_matmul(hb, ssq, w_head, l, 2 * d_lru, 3 * d_fox, BF16)
        proj_dil = scaled_matmul(hb, ssq, w_dil, l, 0, 3 * d_dil, BF16)
        cum, cum_t = forget_cumsum(hb, ssq, w_ff, b_ff, l, B)
        y_a = lru_branch(proj_lru, conv_w[l], conv_b[l], lru_w_a[l], lru_b_a[l], lru_w_x[l], lru_b_x[l],
                         lru_lam[l], out_norm_lru[l], B)
        y_b = fox_attention(proj_fox, 0, n_fox, cum, cum_t, B)
        y_c, mix_w_out_b = dilated_attention(proj_dil, 0, n_dil, B, mix_w_out, l)
        h, hb, ssq = mix_out(y_a, y_b, y_c, out_norm_fox[l], out_norm_dil[l], mix_w_out_b, h)

        act, w_out_b = swiglu_in(hb, ssq, ffn2_norm[l], ffn2_w_in, ffn2_w_out, l)
        h, hb, ssq = ffn_out(act, w_out_b, h, 0.5)
    return scale_norm(h, ssq, final_norm).reshape(B, S, D)
```

```python
import functools
import math

import numpy as np
import jax
import jax.numpy as jnp
from jax import lax
from jax.experimental import pallas as pl
from jax.experimental.pallas import tpu as pltpu

F32 = jnp.float32
BF16 = jnp.bfloat16

LANES = 128
SUBLANES = 8
HEAD_DIM = 128
LRU_BLOCK = 128
CONV_WIDTH = 4
LRU_C = 8.0
ROPE_THETA = 500000.0
ROPE_DIM = HEAD_DIM // 4
DILATED_PATTERNS = ((128, 1), (512, 4), (2048, 16))
EPS = 1e-6
NEG_BIG = -1e30
LOG2E = 1.4426950408889634
SWIGLU_ROW_CHUNK = 256
HEADS_PER_STEP = 2

VMEM_BYTES_V7X = 64 * 1024 * 1024
VMEM_LIMIT_BYTES = VMEM_BYTES_V7X - 8 * 1024 * 1024
VMEM_LIMIT_WIDE_BYTES = VMEM_BYTES_V7X - 2 * 1024 * 1024


def _params(*sem, vmem_limit=None):
    return pltpu.CompilerParams(dimension_semantics=sem, vmem_limit_bytes=vmem_limit or VMEM_LIMIT_BYTES)


def _tile(n, preferred):
    t = min(preferred, n)
    while n % t:
        t //= 2
    assert t % LANES == 0 or t == n, (n, preferred)
    return t


def _lane_partial_sumsq(x):
    x2 = x * x
    acc = x2[:, 0:LANES]
    for k in range(1, x.shape[1] // LANES):
        acc = acc + x2[:, k * LANES:(k + 1) * LANES]
    return acc


def _dot_nt(a, b):
    return lax.dot_general(a, b, (((1,), (1,)), ((), ())), preferred_element_type=F32)


def _row_scale(ssq, width):
    total = jnp.sum(jnp.sum(ssq, axis=0), axis=-1, keepdims=True)
    return lax.rsqrt(total * (1.0 / width) + EPS)


def _ssq_spec(ssq, tm, index):
    return pl.BlockSpec((ssq.shape[0], tm, LANES), lambda *ids: (0, index(*ids), 0))


def _rowstat_kernel(x_ref, hb_ref, ssq_ref):
    x = x_ref[...]
    hb_ref[...] = x.astype(hb_ref.dtype)
    ssq_ref[0] = _lane_partial_sumsq(x)


def rowstat(x, tm=512):
    T, C = x.shape
    tm = min(tm, T)
    return pl.pallas_call(
        _rowstat_kernel,
        grid=(T // tm,),
        in_specs=[pl.BlockSpec((tm, C), lambda i: (i, 0))],
        out_specs=[pl.BlockSpec((tm, C), lambda i: (i, 0)), pl.BlockSpec((1, tm, LANES), lambda i: (0, i, 0))],
        out_shape=[jax.ShapeDtypeStruct((T, C), BF16), jax.ShapeDtypeStruct((1, T, LANES), F32)],
        compiler_params=_params("parallel"),
        name="rowstat",
    )(x)


def _scale_norm_kernel(x_ref, ssq_ref, g_ref, o_ref):
    x = x_ref[...]
    o_ref[...] = (x * _row_scale(ssq_ref[...], x.shape[1]) * g_ref[...]).astype(o_ref.dtype)


def scale_norm(x, ssq, g, tm=512):
    T, C = x.shape
    tm = min(tm, T)
    return pl.pallas_call(
        _scale_norm_kernel,
        grid=(T // tm,),
        in_specs=[pl.BlockSpec((tm, C), lambda i: (i, 0)),
                  _ssq_spec(ssq, tm, lambda i: i),
                  pl.BlockSpec((1, C), lambda i: (0, 0))],
        out_specs=pl.BlockSpec((tm, C), lambda i: (i, 0)),
        out_shape=jax.ShapeDtypeStruct((T, C), x.dtype),
        compiler_params=_params("parallel"),
        name="scale_norm",
    )(x, ssq, g.reshape(1, C).astype(F32))


def _swiglu_in_kernel(x_ref, ssq_ref, gb_ref, wg_ref, wu_ref, wo_ref, o_ref, wob_ref):
    K, tn = wg_ref.shape
    gb = jnp.concatenate([gb_ref[...]] * (tn // LANES), axis=1)
    wg = (wg_ref[...] * gb).astype(BF16)
    wu = (wu_ref[...] * gb).astype(BF16)
    rc = min(SWIGLU_ROW_CHUNK, x_ref.shape[0])
    for r in range(x_ref.shape[0] // rc):
        rows = slice(r * rc, (r + 1) * rc)
        x = x_ref[rows, :]
        rs = _row_scale(ssq_ref[:, rows, :], K)
        g = jnp.dot(x, wg, preferred_element_type=F32) * rs
        u = jnp.dot(x, wu, preferred_element_type=F32) * rs
        o_ref[rows, :] = (g * jax.nn.sigmoid(g) * u).astype(o_ref.dtype)
    wob_ref[...] = wo_ref[...].astype(wob_ref.dtype)


def swiglu_in(hb, ssq, gain, w_in, w_out, layer, tm=2048, tn=256):
    T, K = hb.shape
    tm = min(tm, T)
    if ssq.shape[0] > 1:
        ssq = jnp.sum(ssq, axis=0, keepdims=True)
    F = w_in.shape[2] // 2
    N = w_out.shape[2]
    nf = F // tn
    steps = (T // tm) * nf
    slab = F // steps
    assert F % steps == 0 and slab % 16 == 0, (F, steps)
    gb = jnp.broadcast_to(gain.astype(F32)[:, None], (K, LANES))
    return pl.pallas_call(
        _swiglu_in_kernel,
        grid=(T // tm, nf),
        in_specs=[pl.BlockSpec((tm, K), lambda i, j: (i, 0)),
                  _ssq_spec(ssq, tm, lambda i, j: i),
                  pl.BlockSpec((K, LANES), lambda i, j: (0, 0)),
                  pl.BlockSpec((None, K, tn), lambda i, j: (layer, 0, j)),
                  pl.BlockSpec((None, K, tn), lambda i, j: (layer, 0, j + nf)),
                  pl.BlockSpec((None, slab, N), lambda i, j: (layer, i * nf + j, 0))],
        out_specs=[pl.BlockSpec((tm, tn), lambda i, j: (i, j)),
                   pl.BlockSpec((slab, N), lambda i, j: (i * nf + j, 0))],
        out_shape=[jax.ShapeDtypeStruct((T, F), BF16), jax.ShapeDtypeStruct((F, N), BF16)],
        compiler_params=_params("parallel", "arbitrary", vmem_limit=VMEM_LIMIT_WIDE_BYTES),
        name="swiglu_in",
    )(hb, ssq, gb, w_in, w_in, w_out)


def _emit_stream(h, j, h_ref, hb_ref, ssq_ref):
    h_ref[...] = h
    hb_ref[...] = h.astype(hb_ref.dtype)
    part = _lane_partial_sumsq(h)

    @pl.when(j == 0)
    def _():
        ssq_ref[0] = part

    @pl.when(j != 0)
    def _():
        ssq_ref[0] += part


def _stream_out(T, N, tm, tn):
    specs = [pl.BlockSpec((tm, tn), lambda i, j: (i, j)),
             pl.BlockSpec((tm, tn), lambda i, j: (i, j)),
             pl.BlockSpec((1, tm, LANES), lambda i, j: (0, i, 0))]
    shapes = [jax.ShapeDtypeStruct((T, N), F32), jax.ShapeDtypeStruct((T, N), BF16),
              jax.ShapeDtypeStruct((1, T, LANES), F32)]
    return specs, shapes


def _ffn_out_kernel(a_ref, w_ref, r_ref, h_ref, hb_ref, ssq_ref, *, scale):
    h = r_ref[...] + scale * jnp.dot(a_ref[...], w_ref[...], preferred_element_type=F32)
    h_ref[...] = h
    hb_ref[...] = h.astype(hb_ref.dtype)
    ssq_ref[...] = _lane_partial_sumsq(h)


def ffn_out(a, w, resid, scale, tm=512, tn=1024):
    T, K = a.shape
    tm = min(tm, T)
    N = w.shape[1]
    tn = _tile(N, tn)
    return pl.pallas_call(
        functools.partial(_ffn_out_kernel, scale=scale),
        grid=(N // tn, T // tm),
        in_specs=[pl.BlockSpec((tm, K), lambda j, i: (i, 0)),
                  pl.BlockSpec((K, tn), lambda j, i: (0, j), pipeline_mode=pl.Buffered(1)),
                  pl.BlockSpec((tm, tn), lambda j, i: (i, j))],
        out_specs=[pl.BlockSpec((tm, tn), lambda j, i: (i, j)),
                   pl.BlockSpec((tm, tn), lambda j, i: (i, j)),
                   pl.BlockSpec((None, tm, LANES), lambda j, i: (j, i, 0))],
        out_shape=[jax.ShapeDtypeStruct((T, N), F32), jax.ShapeDtypeStruct((T, N), BF16),
                   jax.ShapeDtypeStruct((N // tn, T, LANES), F32)],
        compiler_params=_params("parallel", "parallel", vmem_limit=VMEM_LIMIT_WIDE_BYTES),
        name="ffn_out",
    )(a, w, resid)


def _mix_out_kernel(ya_ref, yb_ref, yc_ref, gb_ref, gc_ref, w_ref, r_ref, h_ref, hb_ref, ssq_ref, yn_ref):
    j = pl.program_id(1)
    ca, cb = ya_ref.shape[1], yb_ref.shape[1]

    @pl.when(j == 0)
    def _():
        def norm(y_ref, g_ref):
            y = y_ref[...].astype(F32)
            ms = jnp.mean(y * y, axis=-1, keepdims=True)
            return (y * lax.rsqrt(ms + EPS) * g_ref[...]).astype(yn_ref.dtype)
        yn_ref[:, 0:ca] = ya_ref[...]
        yn_ref[:, ca:ca + cb] = norm(yb_ref, gb_ref)
        yn_ref[:, ca + cb:] = norm(yc_ref, gc_ref)

    acc = jnp.dot(yn_ref[...], w_ref[...], preferred_element_type=F32)
    _emit_stream(r_ref[...] + acc, j, h_ref, hb_ref, ssq_ref)


def mix_out(ya, yb, yc, gain_b, gain_c, w, resid, tm=1024, tn=512):
    T, ca = ya.shape
    tm = min(tm, T)
    cb, cc = yb.shape[1], yc.shape[1]
    K, N = w.shape
    tn = _tile(N, tn)
    out_specs, out_shape = _stream_out(T, N, tm, tn)
    rows = lambda c: pl.BlockSpec((tm, c), lambda i, j: (i, 0))
    gain = lambda c: pl.BlockSpec((1, c), lambda i, j: (0, 0))
    return pl.pallas_call(
        _mix_out_kernel,
        grid=(T // tm, N // tn),
        in_specs=[rows(ca), rows(cb), rows(cc), gain(cb), gain(cc),
                  pl.BlockSpec((K, tn), lambda i, j: (0, j)),
                  pl.BlockSpec((tm, tn), lambda i, j: (i, j))],
        out_specs=out_specs,
        out_shape=out_shape,
        scratch_shapes=[pltpu.VMEM((tm, K), BF16)],
        compiler_params=_params("parallel", "arbitrary", vmem_limit=VMEM_LIMIT_WIDE_BYTES),
        name="mix_out",
    )(ya, yb, yc, gain_b.reshape(1, cb).astype(F32), gain_c.reshape(1, cc).astype(F32), w, resid)


def _mix_prep_kernel(wt_ref, g_ref, o_ref, *, q_lo, q_hi, q_scale, transpose):
    i = pl.program_id(1)
    scale = jnp.where((i >= q_lo) & (i < q_hi), q_scale, 1.0).astype(F32)
    w = wt_ref[...] * (g_ref[...] * scale)
    o_ref[...] = (w.T if transpose else w).astype(o_ref.dtype)


def mix_prep(w_t, gain, row0, n_rows, q_rows, q_scale, transpose=True, tr=256):
    L, d_in, K = w_t.shape
    tr = _tile(n_rows, tr)
    assert d_in % SUBLANES == 0 and row0 % SUBLANES == 0 and q_rows[0] % tr == 0 and q_rows[1] % tr == 0
    assert row0 + n_rows <= d_in
    rows = lambda l, i: (pl.multiple_of(l * d_in + row0 + i * tr, SUBLANES), 0)
    if transpose:
        out_spec, out_dims = pl.BlockSpec((None, K, tr), lambda l, i: (l, 0, i)), (L, K, n_rows)
    else:
        out_spec, out_dims = pl.BlockSpec((None, tr, K), lambda l, i: (l, i, 0)), (L, n_rows, K)
    return pl.pallas_call(
        functools.partial(_mix_prep_kernel, q_lo=q_rows[0] // tr, q_hi=q_rows[1] // tr, q_scale=q_scale,
                          transpose=transpose),
        grid=(L, n_rows // tr),
        in_specs=[pl.BlockSpec((pl.Element(tr), pl.Element(K)), rows),
                  pl.BlockSpec((None, 1, K), lambda l, i: (l, 0, 0))],
        out_specs=out_spec,
        out_shape=jax.ShapeDtypeStruct(out_dims, BF16),
        compiler_params=_params("parallel", "parallel"),
        name="mix_prep",
    )(w_t.reshape(L * d_in, K), gain.astype(F32)[:, None, :])


def _scaled_matmul_kernel(a_ref, ssq_ref, w_ref, *rest):
    rs = _row_scale(ssq_ref[...], a_ref.shape[1])
    if len(rest) == 1:
        (o_ref,) = rest
    else:
        wg_ref, bg_ref, o_ref, g_ref = rest

        @pl.when(pl.program_id(1) == 0)
        def _():
            g_ref[...] = _dot_nt(a_ref[...], wg_ref[...]) * rs + bg_ref[...]

    o_ref[...] = (jnp.dot(a_ref[...], w_ref[...], preferred_element_type=F32) * rs).astype(o_ref.dtype)


def scaled_matmul(hb, ssq, w, layer, col0, N, out_dtype, side=None, tm=1024, tn=1024):
    T, K = hb.shape
    tm = min(tm, T)
    tn = _tile(math.gcd(N, col0), tn)
    j0 = col0 // tn
    in_specs = [pl.BlockSpec((tm, K), lambda i, j: (i, 0)),
                _ssq_spec(ssq, tm, lambda i, j: i),
                pl.BlockSpec((None, K, tn), lambda i, j: (layer, 0, j0 + j))]
    out_specs = pl.BlockSpec((tm, tn), lambda i, j: (i, j))
    out_shape = jax.ShapeDtypeStruct((T, N), out_dtype)
    operands = (hb, ssq, w)
    if side is not None:
        P = side[0].shape[1]
        in_specs += [pl.BlockSpec((None, P, K), lambda i, j: (layer, 0, 0)),
                     pl.BlockSpec((None, 1, P), lambda i, j: (layer, 0, 0))]
        out_specs = [out_specs, pl.BlockSpec((tm, P), lambda i, j: (i, 0))]
        out_shape = [out_shape, jax.ShapeDtypeStruct((T, P), F32)]
        operands += tuple(side)
    return pl.pallas_call(
        _scaled_matmul_kernel,
        grid=(T // tm, N // tn),
        in_specs=in_specs,
        out_specs=out_specs,
        out_shape=out_shape,
        compiler_params=_params("parallel", "arbitrary"),
        name="scaled_matmul",
    )(*operands)


def _row_iota(shape):
    return lax.broadcasted_iota(jnp.int32, shape, 0)


def _log_sigmoid(x):
    return jnp.minimum(x, 0.0) - jnp.log1p(jnp.exp(-jnp.abs(x)))


def _softplus(x):
    return jnp.maximum(x, 0.0) + jnp.log1p(jnp.exp(-jnp.abs(x)))


def _forget_cumsum_kernel(z_ref, o_ref, ot_ref, carry_ref):
    @pl.when(pl.program_id(1) == 0)
    def _():
        carry_ref[...] = jnp.zeros_like(carry_ref)

    c = _log_sigmoid(z_ref[...])
    ts = c.shape[0]
    rows = _row_iota(c.shape)
    shift = 1
    while shift < ts:
        c = c + jnp.where(rows >= shift, pltpu.roll(c, shift, 0), 0.0)
        shift *= 2
    c = c + carry_ref[...]
    o_ref[...] = c
    ot_ref[...] = c.T
    carry_ref[...] = c[ts - 1:ts, :]


def forget_cumsum(z, batch, ts=512):
    T, P = z.shape
    S = T // batch
    ts = min(ts, S)
    nc = S // ts
    return pl.pallas_call(
        _forget_cumsum_kernel,
        grid=(batch, nc),
        in_specs=[pl.BlockSpec((ts, P), lambda b, c: (b * nc + c, 0))],
        out_specs=[pl.BlockSpec((ts, P), lambda b, c: (b * nc + c, 0)),
                   pl.BlockSpec((None, P, ts), lambda b, c: (b, 0, c))],
        out_shape=[jax.ShapeDtypeStruct((T, P), F32), jax.ShapeDtypeStruct((batch, P, S), F32)],
        scratch_shapes=[pltpu.VMEM((1, P), F32)],
        compiler_params=_params("parallel", "arbitrary"),
        name="forget_cumsum",
    )(z)


def _lru_kernel(xa_ref, ga_ref, cw_ref, cb_ref, wa_ref, ba_ref, wx_ref, bx_ref, lam_ref, gn_ref,
                o_ref, tail_ref, h_ref):
    @pl.when(pl.program_id(1) == 0)
    def _():
        tail_ref[...] = jnp.zeros_like(tail_ref)
        h_ref[...] = jnp.zeros_like(h_ref)

    x = xa_ref[...]
    ts, C = x.shape
    xb = jnp.concatenate([tail_ref[...], x], axis=0)
    y = cb_ref[...] + cw_ref[CONV_WIDTH - 1:CONV_WIDTH, :] * x
    for back in range(1, CONV_WIDTH):
        y = y + cw_ref[CONV_WIDTH - 1 - back:CONV_WIDTH - back, :] * pltpu.roll(xb, back, 0)[SUBLANES:, :]
    tail_ref[...] = x[ts - SUBLANES:, :]

    rs, gs = [], []
    for g in range(C // LRU_BLOCK):
        yg = y[:, g * LRU_BLOCK:(g + 1) * LRU_BLOCK].astype(BF16)
        rs.append(jnp.dot(yg, wa_ref[g], preferred_element_type=F32))
        gs.append(jnp.dot(yg, wx_ref[g], preferred_element_type=F32))
    r = jax.nn.sigmoid(jnp.concatenate(rs, axis=1) + ba_ref[...])
    i = jax.nn.sigmoid(jnp.concatenate(gs, axis=1) + bx_ref[...])

    log_a = (-LRU_C) * r * _softplus(-lam_ref[...])
    a = jnp.exp(log_a)
    th = jnp.abs(jnp.tanh(log_a))
    u = jnp.sqrt(2.0 * th / (1.0 + th)) * (i * y)

    n_groups = ts // SUBLANES
    a = a.reshape(n_groups, SUBLANES, C)
    u = u.reshape(n_groups, SUBLANES, C)
    sub = lax.broadcasted_iota(jnp.int32, a.shape, 1)
    shift = 1
    while shift < SUBLANES:
        keep = sub >= shift
        a_prev = jnp.where(keep, pltpu.roll(a, shift, 1), 1.0)
        u_prev = jnp.where(keep, pltpu.roll(u, shift, 1), 0.0)
        u = a * u_prev + u
        a = a * a_prev
        shift *= 2
    carry = h_ref[...]
    groups = []
    for g in range(n_groups):
        hg = a[g] * carry + u[g]
        groups.append(hg)
        carry = hg[SUBLANES - 1:SUBLANES, :]
    h = jnp.concatenate(groups, axis=0)
    h_ref[...] = carry

    out = h * jax.nn.gelu(ga_ref[...])
    ms = jnp.mean(out * out, axis=-1, keepdims=True)
    o_ref[...] = (out * lax.rsqrt(ms + EPS) * gn_ref[...]).astype(o_ref.dtype)


def lru_branch(proj, conv_w, conv_b, w_a, b_a, w_x, b_x, lam, gain, batch, ts=256):
    T = proj.shape[0]
    C = proj.shape[1] // 2
    S = T // batch
    ts = min(ts, S)
    nc = S // ts
    row = lambda v: v.reshape(1, C).astype(F32)
    full2 = lambda shape: pl.BlockSpec(shape, lambda b, c: (0, 0))
    full3 = lambda shape: pl.BlockSpec(shape, lambda b, c: (0, 0, 0))
    return pl.pallas_call(
        _lru_kernel,
        grid=(batch, nc),
        in_specs=[pl.BlockSpec((ts, C), lambda b, c: (b * nc + c, 0)),
                  pl.BlockSpec((ts, C), lambda b, c: (b * nc + c, 1)),
                  full2((CONV_WIDTH, C)), full2((1, C)),
                  full3(w_a.shape), full2((1, C)),
                  full3(w_x.shape), full2((1, C)),
                  full2((1, C)), full2((1, C))],
        out_specs=pl.BlockSpec((ts, C), lambda b, c: (b * nc + c, 0)),
        out_shape=jax.ShapeDtypeStruct((T, C), BF16),
        scratch_shapes=[pltpu.VMEM((SUBLANES, C), F32), pltpu.VMEM((1, C), F32)],
        compiler_params=_params("parallel", "arbitrary"),
        name="lru_branch",
    )(proj, proj, conv_w.astype(F32), row(conv_b), w_a.astype(BF16), row(b_a),
      w_x.astype(BF16), row(b_x), row(lam), row(gain))


def _softmax2(z_chunks, row_offset, p_dtype):
    m = z_chunks[0]
    for zc in z_chunks[1:]:
        m = jnp.maximum(m, zc)
    m = jnp.max(m, axis=-1, keepdims=True)
    shift = m if row_offset is None else (m + row_offset) - row_offset
    ps = [jnp.exp2(zc - shift) for zc in z_chunks]
    l = ps[0]
    for p in ps[1:]:
        l = l + p
    l = jnp.sum(l, axis=-1, keepdims=True)
    p = jnp.concatenate(ps, axis=1) if len(ps) > 1 else ps[0]
    return p.astype(p_dtype), l


def _causal_sweep(heads, tq):
    nq = heads[0][0].shape[0] // tq
    items = [(i, head) for i in reversed(range(nq)) for head in heads]

    def scores(i, head):
        q_ref, k_ref = head[0], head[1]
        return _dot_nt(q_ref[i * tq:(i + 1) * tq, :], k_ref[0:(i + 1) * tq, :])

    def probs(i, head, s):
        v_ref, bias_chunks, row_offset = head[2], head[4], head[5]
        z = [bias_chunks(i, c, s[:, c * tq:(c + 1) * tq]) for c in range(i + 1)]
        return _softmax2(z, None if row_offset is None else row_offset(i), v_ref.dtype)

    def values(i, head, p, l):
        v_ref, o_ref = head[2], head[3]
        pv = jnp.dot(p, v_ref[0:(i + 1) * tq, :], preferred_element_type=F32)
        o_ref[i * tq:(i + 1) * tq, :] = (pv / l).astype(o_ref.dtype)

    s_next, pending = scores(*items[0]), None
    for n, item in enumerate(items):
        s = s_next
        if n + 1 < len(items):
            s_next = scores(*items[n + 1])
        p, l = probs(*item, s)
        if pending is not None:
            values(*pending)
        pending = item + (p, l)
    values(*pending)


def _fox_kernel(q_ref, k_ref, v_ref, cum_ref, cumt_ref, tri_ref, o_ref, *, tq):
    n_group = q_ref.shape[1] // HEAD_DIM
    lane = lax.broadcasted_iota(jnp.int32, (tq, cum_ref.shape[1]), 1)
    heads = []
    for h in range(n_group):
        cols = pl.ds(h * HEAD_DIM, HEAD_DIM)
        head = pl.program_id(1) * n_group + h
        ck2 = cumt_ref[pl.ds(head, 1), :] * LOG2E

        def bias_chunks(i, c, s, ck2=ck2):
            z = s - ck2[:, c * tq:(c + 1) * tq]
            return z + tri_ref[...] if c == i else z

        def row_offset(i, head=head):
            cum_t = jnp.where(lane == head, cum_ref[i * tq:(i + 1) * tq, :], 0.0)
            return jnp.sum(cum_t, axis=-1, keepdims=True) * LOG2E

        heads.append((q_ref.at[:, cols], k_ref.at[:, cols], v_ref.at[:, cols], o_ref.at[:, cols],
                      bias_chunks, row_offset))
    _causal_sweep(heads, tq)


def fox_attention(qkv, col0, n_heads, cum, cum_t, batch, tq=256):
    T = qkv.shape[0]
    S = T // batch
    tq = min(tq, S)
    hp = math.gcd(HEADS_PER_STEP, n_heads)
    w = hp * HEAD_DIM
    tri = np.where(np.arange(tq)[:, None] >= np.arange(tq)[None, :], 0.0, NEG_BIG).astype(np.float32)
    assert col0 % hp == 0
    qkv_spec = lambda off: pl.BlockSpec((S, w), lambda b, g: (b, (col0 + off) // hp + g))
    return pl.pallas_call(
        functools.partial(_fox_kernel, tq=tq),
        grid=(batch, n_heads // hp),
        in_specs=[qkv_spec(0), qkv_spec(n_heads), qkv_spec(2 * n_heads),
                  pl.BlockSpec((S, cum.shape[1]), lambda b, g: (b, 0)),
                  pl.BlockSpec((None, cum_t.shape[1], S), lambda b, g: (b, 0, 0)),
                  pl.BlockSpec((tq, tq), lambda b, g: (0, 0))],
        out_specs=pl.BlockSpec((S, w), lambda b, g: (b, g)),
        out_shape=jax.ShapeDtypeStruct((T, n_heads * HEAD_DIM), BF16),
        compiler_params=_params("parallel", "parallel"),
        name="fox_attention",
    )(qkv, qkv, qkv, cum, cum_t, jnp.asarray(tri))


def _dilated_kernel(q_ref, k_ref, v_ref, rc_ref, ra_ref, rb_ref, lm_ref, wo_ref, o_ref, wob_ref, qs_ref, ks_ref,
                    *, tq):
    def rope(x):
        x = x.astype(F32)
        half = ROPE_DIM // 2
        return (x * rc_ref[...] + pltpu.roll(x, HEAD_DIM - half, 1) * ra_ref[...]
                + pltpu.roll(x, half, 1) * rb_ref[...])

    heads = [pl.ds(h * HEAD_DIM, HEAD_DIM) for h in range(q_ref.shape[1] // HEAD_DIM)]
    for cols in heads:
        qs_ref[:, cols] = rope(q_ref[:, cols]).astype(qs_ref.dtype)
        ks_ref[:, cols] = rope(k_ref[:, cols]).astype(ks_ref.dtype)
    add_log_multiplicity = lambda i, c, s: s + lm_ref[i - c]
    _causal_sweep([(qs_ref.at[:, cols], ks_ref.at[:, cols], v_ref.at[:, cols], o_ref.at[:, cols],
                    add_log_multiplicity, None) for cols in heads], tq)
    wob_ref[...] = wo_ref[...].astype(wob_ref.dtype)


def _dilated_log2_multiplicity(S, tq):
    nq = S // tq
    delta = (np.arange(nq)[:, None, None] * tq + np.arange(tq)[None, :, None]
             - np.arange(tq)[None, None, :])
    mult = np.zeros(delta.shape, np.float64)
    for window, dilation in DILATED_PATTERNS:
        mult += (delta >= 0) & (delta <= window) & (delta % dilation == 0)
    with np.errstate(divide="ignore"):
        return np.where(mult > 0, np.log2(mult), NEG_BIG).astype(np.float32)


def _rope_tables(S):
    inv = 1.0 / (ROPE_THETA ** (jnp.arange(0, ROPE_DIM, 2, dtype=F32) / ROPE_DIM))
    ang = jnp.arange(S, dtype=F32)[:, None] * inv[None, :]
    cos, sin = jnp.cos(ang), jnp.sin(ang)
    half = ROPE_DIM // 2
    pad = lambda t, lo: jnp.concatenate(
        [jnp.zeros((S, lo), F32), t, jnp.zeros((S, HEAD_DIM - lo - t.shape[1]), F32)], axis=1)
    rc = jnp.concatenate([cos, cos, jnp.ones((S, HEAD_DIM - ROPE_DIM), F32)], axis=1)
    ra = pad(-sin, 0)
    rb = pad(sin, half)
    return rc, ra, rb


def dilated_attention(qkv, col0, n_heads, batch, w_out, layer, tq=256):
    T = qkv.shape[0]
    S = T // batch
    tq = min(tq, S)
    hd = HEAD_DIM
    nq = S // tq
    hp = math.gcd(HEADS_PER_STEP, n_heads)
    w = hp * hd
    assert col0 % hp == 0
    ng = n_heads // hp
    Kw, Nw = w_out.shape[1], w_out.shape[2]
    slab = Kw // (batch * ng)
    assert Kw % (batch * ng) == 0 and slab % 16 == 0, (Kw, batch, ng)
    rc, ra, rb = _rope_tables(S)
    lm = jnp.asarray(_dilated_log2_multiplicity(S, tq))
    qkv_spec = lambda off: pl.BlockSpec((S, w), lambda b, g: (b, (col0 + off) // hp + g))
    tab = pl.BlockSpec((S, hd), lambda b, g: (0, 0))
    return pl.pallas_call(
        functools.partial(_dilated_kernel, tq=tq),
        grid=(batch, n_heads // hp),
        in_specs=[qkv_spec(0), qkv_spec(n_heads), qkv_spec(2 * n_heads), tab, tab, tab,
                  pl.BlockSpec((nq, tq, tq), lambda b, g: (0, 0, 0)),
                  pl.BlockSpec((None, slab, Nw), lambda b, g: (layer, b * ng + g, 0))],
        out_specs=[pl.BlockSpec((S, w), lambda b, g: (b, g)),
                   pl.BlockSpec((slab, Nw), lambda b, g: (b * ng + g, 0))],
        out_shape=[jax.ShapeDtypeStruct((T, n_heads * hd), BF16), jax.ShapeDtypeStruct((Kw, Nw), BF16)],
        scratch_shapes=[pltpu.VMEM((S, w), BF16), pltpu.VMEM((S, w), BF16)],
        compiler_params=_params("parallel", "parallel"),
        name="dilated_attention",
    )(qkv, qkv, qkv, rc, ra, rb, lm, w_out)


def kernel(x, ffn1_norm, ffn1_w_in, ffn1_w_out, mix_norm, mix_w_in, conv_w, conv_b, lru_w_a, lru_b_a,
           lru_w_x, lru_b_x, lru_lam, fox_b_f, out_norm_lru, out_norm_fox, out_norm_dil, mix_w_out,
           ffn2_norm, ffn2_w_in, ffn2_w_out, final_norm):
    B, S, D = x.shape
    T = B * S
    depth = ffn1_norm.shape[0]
    d_lru = conv_w.shape[2]
    n_fox = fox_b_f.shape[1]
    d_fox = n_fox * HEAD_DIM
    d_dil = out_norm_dil.shape[1]
    n_dil = d_dil // HEAD_DIM

    w_t = jnp.swapaxes(mix_w_in, 1, 2)
    q_scale = HEAD_DIM ** -0.5 * LOG2E
    ff0 = 2 * d_lru + 3 * d_fox
    w_head = mix_prep(w_t, mix_norm, 0, ff0, (2 * d_lru, 2 * d_lru + d_fox), q_scale)
    w_dil = mix_prep(w_t, mix_norm, ff0 + n_fox, 3 * d_dil, (0, d_dil), q_scale)
    w_ff = mix_prep(w_t, mix_norm, ff0, LANES, (0, 0), q_scale, transpose=False)
    b_ff = jnp.pad(fox_b_f.astype(F32), ((0, 0), (0, LANES - n_fox))).reshape(depth, 1, LANES)

    h = x.reshape(T, D)
    hb, ssq = rowstat(h)
    for l in range(depth):
        act, w_out_b = swiglu_in(hb, ssq, ffn1_norm[l], ffn1_w_in, ffn1_w_out, l)
        h, hb, ssq = ffn_out(act, w_out_b, h, 0.5)

        proj_lru = scaled_matmul(hb, ssq, w_head, l, 0, 2 * d_lru, F32)
        proj_fox, gate_logits = scaled_matmul(hb, ssq, w_head, l, 2 * d_lru, 3 * d_fox, BF16, side=(w_ff, b_ff))
        proj_dil = scaled_matmul(hb, ssq, w_dil, l, 0, 3 * d_dil, BF16)
        cum, cum_t = forget_cumsum(gate_logits, B)
        y_a = lru_branch(proj_lru, conv_w[l], conv_b[l], lru_w_a[l], lru_b_a[l], lru_w_x[l], lru_b_x[l],
                         lru_lam[l], out_norm_lru[l], B)
        y_b = fox_attention(proj_fox, 0, n_fox, cum, cum_t, B)
        y_c, mix_w_out_b = dilated_attention(proj_dil, 0, n_dil, B, mix_w_out, l)
        h, hb, ssq = mix_out(y_a, y_b, y_c, out_norm_fox[l], out_norm_dil[l], mix_w_out_b, h)

        act, w_out_b = swiglu_in(hb, ssq, ffn2_norm[l], ffn2_w_in, ffn2_w_out, l)
        h, hb, ssq = ffn_out(act, w_out_b, h, 0.5)
    return scale_norm(h, ssq, final_norm).reshape(B, S, D)
```

```python
import functools
import math

import numpy as np
import jax
import jax.numpy as jnp
from jax import lax
from jax.experimental import pallas as pl
from jax.experimental.pallas import tpu as pltpu

F32 = jnp.float32
BF16 = jnp.bfloat16

LANES = 128
SUBLANES = 8
MXU_K_TILE = 256
HEAD_DIM = 128
LRU_BLOCK = 128
CONV_WIDTH = 4
LRU_C = 8.0
ROPE_THETA = 500000.0
ROPE_DIM = HEAD_DIM // 4
DILATED_PATTERNS = ((128, 1), (512, 4), (2048, 16))
EPS = 1e-6
NEG_BIG = -1e30
LOG2E = 1.4426950408889634
SWIGLU_ROW_CHUNK = 256
HEADS_PER_STEP = 2

VMEM_BYTES_V7X = 64 * 1024 * 1024
VMEM_LIMIT_BYTES = VMEM_BYTES_V7X - 8 * 1024 * 1024
VMEM_LIMIT_WIDE_BYTES = VMEM_BYTES_V7X - 2 * 1024 * 1024


def _params(*sem, vmem_limit=None):
    return pltpu.CompilerParams(dimension_semantics=sem, vmem_limit_bytes=vmem_limit or VMEM_LIMIT_BYTES)


def _tile(n, preferred):
    t = min(preferred, n)
    while n % t:
        t //= 2
    assert t % LANES == 0 or t == n, (n, preferred)
    return t


def _lane_partial_sumsq(x):
    x2 = x * x
    acc = x2[:, 0:LANES]
    for k in range(1, x.shape[1] // LANES):
        acc = acc + x2[:, k * LANES:(k + 1) * LANES]
    return acc


def _dot_nt(a, b):
    return lax.dot_general(a, b, (((1,), (1,)), ((), ())), preferred_element_type=F32)


def _row_scale(ssq, width):
    total = jnp.sum(jnp.sum(ssq, axis=0), axis=-1, keepdims=True)
    return lax.rsqrt(total * (1.0 / width) + EPS)


def _ssq_spec(ssq, tm, index):
    return pl.BlockSpec((ssq.shape[0], tm, LANES), lambda *ids: (0, index(*ids), 0))


def _rowstat_kernel(x_ref, hb_ref, ssq_ref):
    x = x_ref[...]
    hb_ref[...] = x.astype(hb_ref.dtype)
    ssq_ref[0] = _lane_partial_sumsq(x)


def rowstat(x, tm=512):
    T, C = x.shape
    tm = min(tm, T)
    return pl.pallas_call(
        _rowstat_kernel,
        grid=(T // tm,),
        in_specs=[pl.BlockSpec((tm, C), lambda i: (i, 0))],
        out_specs=[pl.BlockSpec((tm, C), lambda i: (i, 0)), pl.BlockSpec((1, tm, LANES), lambda i: (0, i, 0))],
        out_shape=[jax.ShapeDtypeStruct((T, C), BF16), jax.ShapeDtypeStruct((1, T, LANES), F32)],
        compiler_params=_params("parallel"),
        name="rowstat",
    )(x)


def _scale_norm_kernel(x_ref, ssq_ref, g_ref, o_ref):
    x = x_ref[...]
    o_ref[...] = (x * _row_scale(ssq_ref[...], x.shape[1]) * g_ref[...]).astype(o_ref.dtype)


def scale_norm(x, ssq, g, tm=512):
    T, C = x.shape
    tm = min(tm, T)
    return pl.pallas_call(
        _scale_norm_kernel,
        grid=(T // tm,),
        in_specs=[pl.BlockSpec((tm, C), lambda i: (i, 0)),
                  _ssq_spec(ssq, tm, lambda i: i),
                  pl.BlockSpec((1, C), lambda i: (0, 0))],
        out_specs=pl.BlockSpec((tm, C), lambda i: (i, 0)),
        out_shape=jax.ShapeDtypeStruct((T, C), x.dtype),
        compiler_params=_params("parallel"),
        name="scale_norm",
    )(x, ssq, g.reshape(1, C).astype(F32))


def _swiglu_in_kernel(x_ref, ssq_ref, gb_ref, wg_ref, wu_ref, wo_ref, o_ref, wob_ref):
    K, tn = wg_ref.shape
    gb = jnp.concatenate([gb_ref[...]] * (tn // LANES), axis=1)
    wg = (wg_ref[...] * gb).astype(BF16)
    wu = (wu_ref[...] * gb).astype(BF16)
    rc = min(SWIGLU_ROW_CHUNK, x_ref.shape[0])
    for r in range(x_ref.shape[0] // rc):
        rows = slice(r * rc, (r + 1) * rc)
        x = x_ref[rows, :]
        rs = _row_scale(ssq_ref[:, rows, :], K)
        g = jnp.dot(x, wg, preferred_element_type=F32) * rs
        u = jnp.dot(x, wu, preferred_element_type=F32) * rs
        o_ref[rows, :] = (g * jax.nn.sigmoid(g) * u).astype(o_ref.dtype)
    wob_ref[...] = wo_ref[...].astype(wob_ref.dtype)


def swiglu_in(hb, ssq, gain, w_in, w_out, layer, tm=2048, tn=256):
    T, K = hb.shape
    tm = min(tm, T)
    if ssq.shape[0] > 1:
        ssq = jnp.sum(ssq, axis=0, keepdims=True)
    F = w_in.shape[2] // 2
    N = w_out.shape[2]
    nf = F // tn
    steps = (T // tm) * nf
    slab = F // steps
    assert F % steps == 0 and slab % 16 == 0, (F, steps)
    gb = jnp.broadcast_to(gain.astype(F32)[:, None], (K, LANES))
    return pl.pallas_call(
        _swiglu_in_kernel,
        grid=(T // tm, nf),
        in_specs=[pl.BlockSpec((tm, K), lambda i, j: (i, 0)),
                  _ssq_spec(ssq, tm, lambda i, j: i),
                  pl.BlockSpec((K, LANES), lambda i, j: (0, 0)),
                  pl.BlockSpec((None, K, tn), lambda i, j: (layer, 0, j)),
                  pl.BlockSpec((None, K, tn), lambda i, j: (layer, 0, j + nf)),
                  pl.BlockSpec((None, slab, N), lambda i, j: (layer, i * nf + j, 0))],
        out_specs=[pl.BlockSpec((tm, tn), lambda i, j: (i, j)),
                   pl.BlockSpec((slab, N), lambda i, j: (i * nf + j, 0))],
        out_shape=[jax.ShapeDtypeStruct((T, F), BF16), jax.ShapeDtypeStruct((F, N), BF16)],
        compiler_params=_params("parallel", "arbitrary", vmem_limit=VMEM_LIMIT_WIDE_BYTES),
        name="swiglu_in",
    )(hb, ssq, gb, w_in, w_in, w_out)


def _emit_stream(h, j, h_ref, hb_ref, ssq_ref):
    h_ref[...] = h
    hb_ref[...] = h.astype(hb_ref.dtype)
    part = _lane_partial_sumsq(h)

    @pl.when(j == 0)
    def _():
        ssq_ref[0] = part

    @pl.when(j != 0)
    def _():
        ssq_ref[0] += part


def _stream_out(T, N, tm, tn):
    specs = [pl.BlockSpec((tm, tn), lambda i, j: (i, j)),
             pl.BlockSpec((tm, tn), lambda i, j: (i, j)),
             pl.BlockSpec((1, tm, LANES), lambda i, j: (0, i, 0))]
    shapes = [jax.ShapeDtypeStruct((T, N), F32), jax.ShapeDtypeStruct((T, N), BF16),
              jax.ShapeDtypeStruct((1, T, LANES), F32)]
    return specs, shapes


def _ffn_out_kernel(a_ref, wlo_ref, whi_ref, r_ref, h_ref, hb_ref, ssq_ref, *, scale):
    k_lo = wlo_ref.shape[0]
    acc = jnp.dot(a_ref[:, :k_lo], wlo_ref[...], preferred_element_type=F32)
    acc = acc + jnp.dot(a_ref[:, k_lo:], whi_ref[...], preferred_element_type=F32)
    h = r_ref[...] + scale * acc
    h_ref[...] = h
    hb_ref[...] = h.astype(hb_ref.dtype)
    ssq_ref[...] = _lane_partial_sumsq(h)


def ffn_out(a, w, resid, scale, tm=512, tn=1024):
    T, K = a.shape
    tm = min(tm, T)
    N = w.shape[1]
    tn = _tile(N, tn)
    k_lo = (K // MXU_K_TILE + 1) // 2 * MXU_K_TILE
    assert 0 < k_lo < K and (K - k_lo) % 16 == 0, (K, k_lo)
    return pl.pallas_call(
        functools.partial(_ffn_out_kernel, scale=scale),
        grid=(N // tn, T // tm),
        in_specs=[pl.BlockSpec((tm, K), lambda j, i: (i, 0)),
                  pl.BlockSpec((k_lo, tn), lambda j, i: (0, j), pipeline_mode=pl.Buffered(1)),
                  pl.BlockSpec((pl.Element(K - k_lo), pl.Element(tn)), lambda j, i: (k_lo, j * tn),
                               pipeline_mode=pl.Buffered(1)),
                  pl.BlockSpec((tm, tn), lambda j, i: (i, j))],
        out_specs=[pl.BlockSpec((tm, tn), lambda j, i: (i, j)),
                   pl.BlockSpec((tm, tn), lambda j, i: (i, j)),
                   pl.BlockSpec((None, tm, LANES), lambda j, i: (j, i, 0))],
        out_shape=[jax.ShapeDtypeStruct((T, N), F32), jax.ShapeDtypeStruct((T, N), BF16),
                   jax.ShapeDtypeStruct((N // tn, T, LANES), F32)],
        compiler_params=_params("parallel", "parallel", vmem_limit=VMEM_LIMIT_WIDE_BYTES),
        name="ffn_out",
    )(a, w, w, resid)


def _mix_out_kernel(ya_ref, yb_ref, yc_ref, gb_ref, gc_ref, w_ref, r_ref, h_ref, hb_ref, ssq_ref, yn_ref):
    j = pl.program_id(1)
    ca, cb = ya_ref.shape[1], yb_ref.shape[1]

    @pl.when(j == 0)
    def _():
        def norm(y_ref, g_ref):
            y = y_ref[...].astype(F32)
            ms = jnp.mean(y * y, axis=-1, keepdims=True)
            return (y * lax.rsqrt(ms + EPS) * g_ref[...]).astype(yn_ref.dtype)
        yn_ref[:, 0:ca] = ya_ref[...]
        yn_ref[:, ca:ca + cb] = norm(yb_ref, gb_ref)
        yn_ref[:, ca + cb:] = norm(yc_ref, gc_ref)

    acc = jnp.dot(yn_ref[...], w_ref[...], preferred_element_type=F32)
    _emit_stream(r_ref[...] + acc, j, h_ref, hb_ref, ssq_ref)


def mix_out(ya, yb, yc, gain_b, gain_c, w, resid, tm=1024, tn=512):
    T, ca = ya.shape
    tm = min(tm, T)
    cb, cc = yb.shape[1], yc.shape[1]
    K, N = w.shape
    tn = _tile(N, tn)
    out_specs, out_shape = _stream_out(T, N, tm, tn)
    rows = lambda c: pl.BlockSpec((tm, c), lambda i, j: (i, 0))
    gain = lambda c: pl.BlockSpec((1, c), lambda i, j: (0, 0))
    return pl.pallas_call(
        _mix_out_kernel,
        grid=(T // tm, N // tn),
        in_specs=[rows(ca), rows(cb), rows(cc), gain(cb), gain(cc),
                  pl.BlockSpec((K, tn), lambda i, j: (0, j)),
                  pl.BlockSpec((tm, tn), lambda i, j: (i, j))],
        out_specs=out_specs,
        out_shape=out_shape,
        scratch_shapes=[pltpu.VMEM((tm, K), BF16)],
        compiler_params=_params("parallel", "arbitrary", vmem_limit=VMEM_LIMIT_WIDE_BYTES),
        name="mix_out",
    )(ya, yb, yc, gain_b.reshape(1, cb).astype(F32), gain_c.reshape(1, cc).astype(F32), w, resid)


def _mix_prep_kernel(wt_ref, g_ref, o_ref, *, q_lo, q_hi, q_scale, transpose):
    i = pl.program_id(1)
    scale = jnp.where((i >= q_lo) & (i < q_hi), q_scale, 1.0).astype(F32)
    w = wt_ref[...] * (g_ref[...] * scale)
    o_ref[...] = (w.T if transpose else w).astype(o_ref.dtype)


def mix_prep(w_t, gain, row0, n_rows, q_rows, q_scale, transpose=True, tr=256):
    L, d_in, K = w_t.shape
    tr = _tile(n_rows, tr)
    assert d_in % SUBLANES == 0 and row0 % SUBLANES == 0 and q_rows[0] % tr == 0 and q_rows[1] % tr == 0
    assert row0 + n_rows <= d_in
    rows = lambda l, i: (pl.multiple_of(l * d_in + row0 + i * tr, SUBLANES), 0)
    if transpose:
        out_spec, out_dims = pl.BlockSpec((None, K, tr), lambda l, i: (l, 0, i)), (L, K, n_rows)
    else:
        out_spec, out_dims = pl.BlockSpec((None, tr, K), lambda l, i: (l, i, 0)), (L, n_rows, K)
    return pl.pallas_call(
        functools.partial(_mix_prep_kernel, q_lo=q_rows[0] // tr, q_hi=q_rows[1] // tr, q_scale=q_scale,
                          transpose=transpose),
        grid=(L, n_rows // tr),
        in_specs=[pl.BlockSpec((pl.Element(tr), pl.Element(K)), rows),
                  pl.BlockSpec((None, 1, K), lambda l, i: (l, 0, 0))],
        out_specs=out_spec,
        out_shape=jax.ShapeDtypeStruct(out_dims, BF16),
        compiler_params=_params("parallel", "parallel"),
        name="mix_prep",
    )(w_t.reshape(L * d_in, K), gain.astype(F32)[:, None, :])


def _scaled_matmul_kernel(a_ref, ssq_ref, w_ref, o_ref):
    rs = _row_scale(ssq_ref[...], a_ref.shape[1])
    o_ref[...] = (jnp.dot(a_ref[...], w_ref[...], preferred_element_type=F32) * rs).astype(o_ref.dtype)


def scaled_matmul(hb, ssq, w, layer, col0, N, out_dtype, tm=1024, tn=1024):
    T, K = hb.shape
    tm = min(tm, T)
    tn = _tile(math.gcd(N, col0), tn)
    j0 = col0 // tn
    return pl.pallas_call(
        _scaled_matmul_kernel,
        grid=(T // tm, N // tn),
        in_specs=[pl.BlockSpec((tm, K), lambda i, j: (i, 0)),
                  _ssq_spec(ssq, tm, lambda i, j: i),
                  pl.BlockSpec((None, K, tn), lambda i, j: (layer, 0, j0 + j))],
        out_specs=pl.BlockSpec((tm, tn), lambda i, j: (i, j)),
        out_shape=jax.ShapeDtypeStruct((T, N), out_dtype),
        compiler_params=_params("parallel", "arbitrary"),
        name="scaled_matmul",
    )(hb, ssq, w)


def _row_iota(shape):
    return lax.broadcasted_iota(jnp.int32, shape, 0)


def _log_sigmoid(x):
    return jnp.minimum(x, 0.0) - jnp.log1p(jnp.exp(-jnp.abs(x)))


def _softplus(x):
    return jnp.maximum(x, 0.0) + jnp.log1p(jnp.exp(-jnp.abs(x)))


def _forget_cumsum_kernel(x_ref, ssq_ref, w_ref, b_ref, o_ref, ot_ref, carry_ref):
    @pl.when(pl.program_id(1) == 0)
    def _():
        carry_ref[...] = jnp.zeros_like(carry_ref)

    rs = _row_scale(ssq_ref[...], x_ref.shape[1])
    z = _dot_nt(x_ref[...], w_ref[...]) * rs + b_ref[...]
    c = _log_sigmoid(z)
    ts = c.shape[0]
    rows = _row_iota(c.shape)
    shift = 1
    while shift < ts:
        c = c + jnp.where(rows >= shift, pltpu.roll(c, shift, 0), 0.0)
        shift *= 2
    c = c + carry_ref[...]
    o_ref[...] = c
    ot_ref[...] = c.T
    carry_ref[...] = c[ts - 1:ts, :]


def forget_cumsum(hb, ssq, w_ff, b_ff, layer, batch, ts=512):
    T, K = hb.shape
    S = T // batch
    ts = min(ts, S)
    nc = S // ts
    P = w_ff.shape[1]
    return pl.pallas_call(
        _forget_cumsum_kernel,
        grid=(batch, nc),
        in_specs=[pl.BlockSpec((ts, K), lambda b, c: (b * nc + c, 0)),
                  _ssq_spec(ssq, ts, lambda b, c: b * nc + c),
                  pl.BlockSpec((None, P, K), lambda b, c: (layer, 0, 0)),
                  pl.BlockSpec((None, 1, P), lambda b, c: (layer, 0, 0))],
        out_specs=[pl.BlockSpec((ts, P), lambda b, c: (b * nc + c, 0)),
                   pl.BlockSpec((None, P, ts), lambda b, c: (b, 0, c))],
        out_shape=[jax.ShapeDtypeStruct((T, P), F32), jax.ShapeDtypeStruct((batch, P, S), F32)],
        scratch_shapes=[pltpu.VMEM((1, P), F32)],
        compiler_params=_params("parallel", "arbitrary"),
        name="forget_cumsum",
    )(hb, ssq, w_ff, b_ff)


def _lru_kernel(xa_ref, ga_ref, cw_ref, cb_ref, wa_ref, ba_ref, wx_ref, bx_ref, lam_ref, gn_ref,
                o_ref, tail_ref, h_ref):
    @pl.when(pl.program_id(1) == 0)
    def _():
        tail_ref[...] = jnp.zeros_like(tail_ref)
        h_ref[...] = jnp.zeros_like(h_ref)

    x = xa_ref[...]
    ts, C = x.shape
    xb = jnp.concatenate([tail_ref[...], x], axis=0)
    y = cb_ref[...] + cw_ref[CONV_WIDTH - 1:CONV_WIDTH, :] * x
    for back in range(1, CONV_WIDTH):
        y = y + cw_ref[CONV_WIDTH - 1 - back:CONV_WIDTH - back, :] * pltpu.roll(xb, back, 0)[SUBLANES:, :]
    tail_ref[...] = x[ts - SUBLANES:, :]

    rs, gs = [], []
    for g in range(C // LRU_BLOCK):
        yg = y[:, g * LRU_BLOCK:(g + 1) * LRU_BLOCK].astype(BF16)
        rs.append(jnp.dot(yg, wa_ref[g], preferred_element_type=F32))
        gs.append(jnp.dot(yg, wx_ref[g], preferred_element_type=F32))
    r = jax.nn.sigmoid(jnp.concatenate(rs, axis=1) + ba_ref[...])
    i = jax.nn.sigmoid(jnp.concatenate(gs, axis=1) + bx_ref[...])

    log_a = (-LRU_C) * r * _softplus(-lam_ref[...])
    a = jnp.exp(log_a)
    th = jnp.abs(jnp.tanh(log_a))
    u = jnp.sqrt(2.0 * th / (1.0 + th)) * (i * y)

    n_groups = ts // SUBLANES
    a = a.reshape(n_groups, SUBLANES, C)
    u = u.reshape(n_groups, SUBLANES, C)
    sub = lax.broadcasted_iota(jnp.int32, a.shape, 1)
    shift = 1
    while shift < SUBLANES:
        keep = sub >= shift
        a_prev = jnp.where(keep, pltpu.roll(a, shift, 1), 1.0)
        u_prev = jnp.where(keep, pltpu.roll(u, shift, 1), 0.0)
        u = a * u_prev + u
        a = a * a_prev
        shift *= 2
    carry = h_ref[...]
    groups = []
    for g in range(n_groups):
        hg = a[g] * carry + u[g]
        groups.append(hg)
        carry = hg[SUBLANES - 1:SUBLANES, :]
    h = jnp.concatenate(groups, axis=0)
    h_ref[...] = carry

    out = h * jax.nn.gelu(ga_ref[...])
    ms = jnp.mean(out * out, axis=-1, keepdims=True)
    o_ref[...] = (out * lax.rsqrt(ms + EPS) * gn_ref[...]).astype(o_ref.dtype)


def lru_branch(proj, conv_w, conv_b, w_a, b_a, w_x, b_x, lam, gain, batch, ts=512):
    T = proj.shape[0]
    C = proj.shape[1] // 2
    S = T // batch
    ts = min(ts, S)
    nc = S // ts
    row = lambda v: v.reshape(1, C).astype(F32)
    full2 = lambda shape: pl.BlockSpec(shape, lambda b, c: (0, 0))
    full3 = lambda shape: pl.BlockSpec(shape, lambda b, c: (0, 0, 0))
    return pl.pallas_call(
        _lru_kernel,
        grid=(batch, nc),
        in_specs=[pl.BlockSpec((ts, C), lambda b, c: (b * nc + c, 0)),
                  pl.BlockSpec((ts, C), lambda b, c: (b * nc + c, 1)),
                  full2((CONV_WIDTH, C)), full2((1, C)),
                  full3(w_a.shape), full2((1, C)),
                  full3(w_x.shape), full2((1, C)),
                  full2((1, C)), full2((1, C))],
        out_specs=pl.BlockSpec((ts, C), lambda b, c: (b * nc + c, 0)),
        out_shape=jax.ShapeDtypeStruct((T, C), BF16),
        scratch_shapes=[pltpu.VMEM((SUBLANES, C), F32), pltpu.VMEM((1, C), F32)],
        compiler_params=_params("parallel", "arbitrary"),
        name="lru_branch",
    )(proj, proj, conv_w.astype(F32), row(conv_b), w_a.astype(BF16), row(b_a),
      w_x.astype(BF16), row(b_x), row(lam), row(gain))


def _softmax2(z_chunks, row_offset, p_dtype):
    m = z_chunks[0]
    for zc in z_chunks[1:]:
        m = jnp.maximum(m, zc)
    m = jnp.max(m, axis=-1, keepdims=True)
    shift = m if row_offset is None else (m + row_offset) - row_offset
    ps = [jnp.exp2(zc - shift) for zc in z_chunks]
    l = ps[0]
    for p in ps[1:]:
        l = l + p
    l = jnp.sum(l, axis=-1, keepdims=True)
    p = jnp.concatenate(ps, axis=1) if len(ps) > 1 else ps[0]
    return p.astype(p_dtype), l


def _causal_sweep(heads, tq):
    nq = heads[0][0].shape[0] // tq
    items = [(i, head) for i in reversed(range(nq)) for head in heads]

    def scores(i, head):
        q_ref, k_ref = head[0], head[1]
        return _dot_nt(q_ref[i * tq:(i + 1) * tq, :], k_ref[0:(i + 1) * tq, :])

    def probs(i, head, s):
        v_ref, bias_chunks, row_offset = head[2], head[4], head[5]
        z = [bias_chunks(i, c, s[:, c * tq:(c + 1) * tq]) for c in range(i + 1)]
        return _softmax2(z, None if row_offset is None else row_offset(i), v_ref.dtype)

    def values(i, head, p, l):
        v_ref, o_ref = head[2], head[3]
        pv = jnp.dot(p, v_ref[0:(i + 1) * tq, :], preferred_element_type=F32)
        o_ref[i * tq:(i + 1) * tq, :] = (pv / l).astype(o_ref.dtype)

    s_next, pending = scores(*items[0]), None
    for n, item in enumerate(items):
        s = s_next
        if n + 1 < len(items):
            s_next = scores(*items[n + 1])
        p, l = probs(*item, s)
        if pending is not None:
            values(*pending)
        pending = item + (p, l)
    values(*pending)


def _fox_kernel(q_ref, k_ref, v_ref, cum_ref, cumt_ref, tri_ref, o_ref, *, tq):
    n_group = q_ref.shape[1] // HEAD_DIM
    lane = lax.broadcasted_iota(jnp.int32, (tq, cum_ref.shape[1]), 1)
    heads = []
    for h in range(n_group):
        cols = pl.ds(h * HEAD_DIM, HEAD_DIM)
        head = pl.program_id(1) * n_group + h
        ck2 = cumt_ref[pl.ds(head, 1), :] * LOG2E

        def bias_chunks(i, c, s, ck2=ck2):
            z = s - ck2[:, c * tq:(c + 1) * tq]
            return z + tri_ref[...] if c == i else z

        def row_offset(i, head=head):
            cum_t = jnp.where(lane == head, cum_ref[i * tq:(i + 1) * tq, :], 0.0)
            return jnp.sum(cum_t, axis=-1, keepdims=True) * LOG2E

        heads.append((q_ref.at[:, cols], k_ref.at[:, cols], v_ref.at[:, cols], o_ref.at[:, cols],
                      bias_chunks, row_offset))
    _causal_sweep(heads, tq)


def fox_attention(qkv, col0, n_heads, cum, cum_t, batch, tq=256):
    T = qkv.shape[0]
    S = T // batch
    tq = min(tq, S)
    hp = math.gcd(HEADS_PER_STEP, n_heads)
    w = hp * HEAD_DIM
    tri = np.where(np.arange(tq)[:, None] >= np.arange(tq)[None, :], 0.0, NEG_BIG).astype(np.float32)
    assert col0 % hp == 0
    qkv_spec = lambda off: pl.BlockSpec((S, w), lambda b, g: (b, (col0 + off) // hp + g))
    return pl.pallas_call(
        functools.partial(_fox_kernel, tq=tq),
        grid=(batch, n_heads // hp),
        in_specs=[qkv_spec(0), qkv_spec(n_heads), qkv_spec(2 * n_heads),
                  pl.BlockSpec((S, cum.shape[1]), lambda b, g: (b, 0)),
                  pl.BlockSpec((None, cum_t.shape[1], S), lambda b, g: (b, 0, 0)),
                  pl.BlockSpec((tq, tq), lambda b, g: (0, 0))],
        out_specs=pl.BlockSpec((S, w), lambda b, g: (b, g)),
        out_shape=jax.ShapeDtypeStruct((T, n_heads * HEAD_DIM), BF16),
        compiler_params=_params("parallel", "parallel"),
        name="fox_attention",
    )(qkv, qkv, qkv, cum, cum_t, jnp.asarray(tri))


def _dilated_kernel(q_ref, k_ref, v_ref, rc_ref, ra_ref, rb_ref, lm_ref, wo_ref, o_ref, wob_ref, qs_ref, ks_ref,
                    *, tq):
    def rope(x):
        x = x.astype(F32)
        half = ROPE_DIM // 2
        return (x * rc_ref[...] + pltpu.roll(x, HEAD_DIM - half, 1) * ra_ref[...]
                + pltpu.roll(x, half, 1) * rb_ref[...])

    heads = [pl.ds(h * HEAD_DIM, HEAD_DIM) for h in range(q_ref.shape[1] // HEAD_DIM)]
    for cols in heads:
        qs_ref[:, cols] = rope(q_ref[:, cols]).astype(qs_ref.dtype)
        ks_ref[:, cols] = rope(k_ref[:, cols]).astype(ks_ref.dtype)
    add_log_multiplicity = lambda i, c, s: s + lm_ref[i - c]
    _causal_sweep([(qs_ref.at[:, cols], ks_ref.at[:, cols], v_ref.at[:, cols], o_ref.at[:, cols],
                    add_log_multiplicity, None) for cols in heads], tq)
    wob_ref[...] = wo_ref[...].astype(wob_ref.dtype)


def _dilated_log2_multiplicity(S, tq):
    nq = S // tq
    delta = (np.arange(nq)[:, None, None] * tq + np.arange(tq)[None, :, None]
             - np.arange(tq)[None, None, :])
    mult = np.zeros(delta.shape, np.float64)
    for window, dilation in DILATED_PATTERNS:
        mult += (delta >= 0) & (delta <= window) & (delta % dilation == 0)
    with np.errstate(divide="ignore"):
        return np.where(mult > 0, np.log2(mult), NEG_BIG).astype(np.float32)


def _rope_tables(S):
    inv = 1.0 / (ROPE_THETA ** (jnp.arange(0, ROPE_DIM, 2, dtype=F32) / ROPE_DIM))
    ang = jnp.arange(S, dtype=F32)[:, None] * inv[None, :]
    cos, sin = jnp.cos(ang), jnp.sin(ang)
    half = ROPE_DIM // 2
    pad = lambda t, lo: jnp.concatenate(
        [jnp.zeros((S, lo), F32), t, jnp.zeros((S, HEAD_DIM - lo - t.shape[1]), F32)], axis=1)
    rc = jnp.concatenate([cos, cos, jnp.ones((S, HEAD_DIM - ROPE_DIM), F32)], axis=1)
    ra = pad(-sin, 0)
    rb = pad(sin, half)
    return rc, ra, rb


def dilated_attention(qkv, col0, n_heads, batch, w_out, layer, tq=256):
    T = qkv.shape[0]
    S = T // batch
    tq = min(tq, S)
    hd = HEAD_DIM
    nq = S // tq
    hp = math.gcd(HEADS_PER_STEP, n_heads)
    w = hp * hd
    assert col0 % hp == 0
    ng = n_heads // hp
    Kw, Nw = w_out.shape[1], w_out.shape[2]
    slab = Kw // (batch * ng)
    assert Kw % (batch * ng) == 0 and slab % 16 == 0, (Kw, batch, ng)
    rc, ra, rb = _rope_tables(S)
    lm = jnp.asarray(_dilated_log2_multiplicity(S, tq))
    qkv_spec = lambda off: pl.BlockSpec((S, w), lambda b, g: (b, (col0 + off) // hp + g))
    tab = pl.BlockSpec((S, hd), lambda b, g: (0, 0))
    return pl.pallas_call(
        functools.partial(_dilated_kernel, tq=tq),
        grid=(batch, n_heads // hp),
        in_specs=[qkv_spec(0), qkv_spec(n_heads), qkv_spec(2 * n_heads), tab, tab, tab,
                  pl.BlockSpec((nq, tq, tq), lambda b, g: (0, 0, 0)),
                  pl.BlockSpec((None, slab, Nw), lambda b, g: (layer, b * ng + g, 0))],
        out_specs=[pl.BlockSpec((S, w), lambda b, g: (b, g)),
                   pl.BlockSpec((slab, Nw), lambda b, g: (b * ng + g, 0))],
        out_shape=[jax.ShapeDtypeStruct((T, n_heads * hd), BF16), jax.ShapeDtypeStruct((Kw, Nw), BF16)],
        scratch_shapes=[pltpu.VMEM((S, w), BF16), pltpu.VMEM((S, w), BF16)],
        compiler_params=_params("parallel", "parallel"),
        name="dilated_attention",
    )(qkv, qkv, qkv, rc, ra, rb, lm, w_out)


def kernel(x, ffn1_norm, ffn1_w_in, ffn1_w_out, mix_norm, mix_w_in, conv_w, conv_b, lru_w_a, lru_b_a,
           lru_w_x, lru_b_x, lru_lam, fox_b_f, out_norm_lru, out_norm_fox, out_norm_dil, mix_w_out,
           ffn2_norm, ffn2_w_in, ffn2_w_out, final_norm):
    B, S, D = x.shape
    T = B * S
    depth = ffn1_norm.shape[0]
    d_lru = conv_w.shape[2]
    n_fox = fox_b_f.shape[1]
    d_fox = n_fox * HEAD_DIM
    d_dil = out_norm_dil.shape[1]
    n_dil = d_dil // HEAD_DIM

    w_t = jnp.swapaxes(mix_w_in, 1, 2)
    q_scale = HEAD_DIM ** -0.5 * LOG2E
    ff0 = 2 * d_lru + 3 * d_fox
    w_head = mix_prep(w_t, mix_norm, 0, ff0, (2 * d_lru, 2 * d_lru + d_fox), q_scale)
    w_dil = mix_prep(w_t, mix_norm, ff0 + n_fox, 3 * d_dil, (0, d_dil), q_scale)
    w_ff = mix_prep(w_t, mix_norm, ff0, LANES, (0, 0), q_scale, transpose=False)
    b_ff = jnp.pad(fox_b_f.astype(F32), ((0, 0), (0, LANES - n_fox))).reshape(depth, 1, LANES)

    h = x.reshape(T, D)
    hb, ssq = rowstat(h)
    for l in range(depth):
        act, w_out_b = swiglu_in(hb, ssq, ffn1_norm[l], ffn1_w_in, ffn1_w_out, l)
        h, hb, ssq = ffn_out(act, w_out_b, h, 0.5)

        proj_lru = scaled_matmul(hb, ssq, w_head, l, 0, 2 * d_lru, F32)
        proj_fox = scaled_matmul(hb, ssq, w_head, l, 2 * d_lru, 3 * d_fox, BF16)
        proj_dil = scaled_matmul(hb, ssq, w_dil, l, 0, 3 * d_dil, BF16)
        cum, cum_t = forget_cumsum(hb, ssq, w_ff, b_ff, l, B)
        y_a = lru_branch(proj_lru, conv_w[l], conv_b[l], lru_w_a[l], lru_b_a[l], lru_w_x[l], lru_b_x[l],
                         lru_lam[l], out_norm_lru[l], B)
        y_b = fox_attention(proj_fox, 0, n_fox, cum, cum_t, B)
        y_c, mix_w_out_b = dilated_attention(proj_dil, 0, n_dil, B, mix_w_out, l)
        h, hb, ssq = mix_out(y_a, y_b, y_c, out_norm_fox[l], out_norm_dil[l], mix_w_out_b, h)

        act, w_out_b = swiglu_in(hb, ssq, ffn2_norm[l], ffn2_w_in, ffn2_w_out, l)
        h, hb, ssq = ffn_out(act, w_out_b, h, 0.5)
    return scale_norm(h, ssq, final_norm).reshape(B, S, D)
```

```python
import functools
import math

import numpy as np
import jax
import jax.numpy as jnp
from jax import lax
from jax.experimental import pallas as pl
from jax.experimental.pallas import tpu as pltpu

F32 = jnp.float32
BF16 = jnp.bfloat16

LANES = 128
SUBLANES = 8
MXU_K_TILE = 256
HEAD_DIM = 128
LRU_BLOCK = 128
CONV_WIDTH = 4
LRU_C = 8.0
ROPE_THETA = 500000.0
ROPE_DIM = HEAD_DIM // 4
DILATED_PATTERNS = ((128, 1), (512, 4), (2048, 16))
EPS = 1e-6
NEG_BIG = -1e30
LOG2E = 1.4426950408889634
SWIGLU_ROW_CHUNK = 256
HEADS_PER_STEP = 2

VMEM_BYTES_V7X = 64 * 1024 * 1024
VMEM_LIMIT_BYTES = VMEM_BYTES_V7X - 8 * 1024 * 1024
VMEM_LIMIT_WIDE_BYTES = VMEM_BYTES_V7X - 2 * 1024 * 1024


def _params(*sem, vmem_limit=None):
    return pltpu.CompilerParams(dimension_semantics=sem, vmem_limit_bytes=vmem_limit or VMEM_LIMIT_BYTES)


def _tile(n, preferred):
    t = min(preferred, n)
    while n % t:
        t //= 2
    assert t % LANES == 0 or t == n, (n, preferred)
    return t


def _lane_partial_sumsq(x):
    x2 = x * x
    acc = x2[:, 0:LANES]
    for k in range(1, x.shape[1] // LANES):
        acc = acc + x2[:, k * LANES:(k + 1) * LANES]
    return acc


def _dot_nt(a, b):
    return lax.dot_general(a, b, (((1,), (1,)), ((), ())), preferred_element_type=F32)


def _row_scale(ssq, width):
    total = jnp.sum(jnp.sum(ssq, axis=0), axis=-1, keepdims=True)
    return lax.rsqrt(total * (1.0 / width) + EPS)


def _ssq_spec(ssq, tm, index):
    return pl.BlockSpec((ssq.shape[0], tm, LANES), lambda *ids: (0, index(*ids), 0))


def _rowstat_kernel(x_ref, hb_ref, ssq_ref):
    x = x_ref[...]
    hb_ref[...] = x.astype(hb_ref.dtype)
    ssq_ref[0] = _lane_partial_sumsq(x)


def rowstat(x, tm=512):
    T, C = x.shape
    tm = min(tm, T)
    return pl.pallas_call(
        _rowstat_kernel,
        grid=(T // tm,),
        in_specs=[pl.BlockSpec((tm, C), lambda i: (i, 0))],
        out_specs=[pl.BlockSpec((tm, C), lambda i: (i, 0)), pl.BlockSpec((1, tm, LANES), lambda i: (0, i, 0))],
        out_shape=[jax.ShapeDtypeStruct((T, C), BF16), jax.ShapeDtypeStruct((1, T, LANES), F32)],
        compiler_params=_params("parallel"),
        name="rowstat",
    )(x)


def _scale_norm_kernel(x_ref, ssq_ref, g_ref, o_ref):
    x = x_ref[...]
    o_ref[...] = (x * _row_scale(ssq_ref[...], x.shape[1]) * g_ref[...]).astype(o_ref.dtype)


def scale_norm(x, ssq, g, tm=512):
    T, C = x.shape
    tm = min(tm, T)
    return pl.pallas_call(
        _scale_norm_kernel,
        grid=(T // tm,),
        in_specs=[pl.BlockSpec((tm, C), lambda i: (i, 0)),
                  _ssq_spec(ssq, tm, lambda i: i),
                  pl.BlockSpec((1, C), lambda i: (0, 0))],
        out_specs=pl.BlockSpec((tm, C), lambda i: (i, 0)),
        out_shape=jax.ShapeDtypeStruct((T, C), x.dtype),
        compiler_params=_params("parallel"),
        name="scale_norm",
    )(x, ssq, g.reshape(1, C).astype(F32))


def _swiglu_in_kernel(x_ref, ssq_ref, gb_ref, wg_ref, wu_ref, wo_ref, o_ref, wob_ref):
    K, tn = wg_ref.shape
    gb = jnp.concatenate([gb_ref[...]] * (tn // LANES), axis=1)
    wg = (wg_ref[...] * gb).astype(BF16)
    wu = (wu_ref[...] * gb).astype(BF16)
    rc = min(SWIGLU_ROW_CHUNK, x_ref.shape[0])
    for r in range(x_ref.shape[0] // rc):
        rows = slice(r * rc, (r + 1) * rc)
        x = x_ref[rows, :]
        rs = _row_scale(ssq_ref[:, rows, :], K)
        g = jnp.dot(x, wg, preferred_element_type=F32) * rs
        u = jnp.dot(x, wu, preferred_element_type=F32) * rs
        o_ref[rows, :] = (g * jax.nn.sigmoid(g) * u).astype(o_ref.dtype)
    wob_ref[...] = wo_ref[...].astype(wob_ref.dtype)


def swiglu_in(hb, ssq, gain, w_in, w_out, layer, tm=2048, tn=256):
    T, K = hb.shape
    tm = min(tm, T)
    if ssq.shape[0] > 1:
        ssq = jnp.sum(ssq, axis=0, keepdims=True)
    F = w_in.shape[2] // 2
    N = w_out.shape[2]
    nf = F // tn
    steps = (T // tm) * nf
    slab = F // steps
    assert F % steps == 0 and slab % 16 == 0, (F, steps)
    gb = jnp.broadcast_to(gain.astype(F32)[:, None], (K, LANES))
    return pl.pallas_call(
        _swiglu_in_kernel,
        grid=(T // tm, nf),
        in_specs=[pl.BlockSpec((tm, K), lambda i, j: (i, 0)),
                  _ssq_spec(ssq, tm, lambda i, j: i),
                  pl.BlockSpec((K, LANES), lambda i, j: (0, 0)),
                  pl.BlockSpec((None, K, tn), lambda i, j: (layer, 0, j)),
                  pl.BlockSpec((None, K, tn), lambda i, j: (layer, 0, j + nf)),
                  pl.BlockSpec((None, slab, N), lambda i, j: (layer, i * nf + j, 0))],
        out_specs=[pl.BlockSpec((tm, tn), lambda i, j: (i, j)),
                   pl.BlockSpec((slab, N), lambda i, j: (i * nf + j, 0))],
        out_shape=[jax.ShapeDtypeStruct((T, F), BF16), jax.ShapeDtypeStruct((F, N), BF16)],
        compiler_params=_params("parallel", "arbitrary", vmem_limit=VMEM_LIMIT_WIDE_BYTES),
        name="swiglu_in",
    )(hb, ssq, gb, w_in, w_in, w_out)


def _emit_stream(h, j, h_ref, hb_ref, ssq_ref):
    h_ref[...] = h
    hb_ref[...] = h.astype(hb_ref.dtype)
    part = _lane_partial_sumsq(h)

    @pl.when(j == 0)
    def _():
        ssq_ref[0] = part

    @pl.when(j != 0)
    def _():
        ssq_ref[0] += part


def _stream_out(T, N, tm, tn):
    specs = [pl.BlockSpec((tm, tn), lambda i, j: (i, j)),
             pl.BlockSpec((tm, tn), lambda i, j: (i, j)),
             pl.BlockSpec((1, tm, LANES), lambda i, j: (0, i, 0))]
    shapes = [jax.ShapeDtypeStruct((T, N), F32), jax.ShapeDtypeStruct((T, N), BF16),
              jax.ShapeDtypeStruct((1, T, LANES), F32)]
    return specs, shapes


def _ffn_out_kernel(a_ref, wlo_ref, whi_ref, r_ref, h_ref, hb_ref, ssq_ref, *, scale):
    k_lo = wlo_ref.shape[0]
    acc = jnp.dot(a_ref[:, :k_lo], wlo_ref[...], preferred_element_type=F32)
    acc = acc + jnp.dot(a_ref[:, k_lo:], whi_ref[...], preferred_element_type=F32)
    h = r_ref[...] + scale * acc
    h_ref[...] = h
    hb_ref[...] = h.astype(hb_ref.dtype)
    ssq_ref[...] = _lane_partial_sumsq(h)


def ffn_out(a, w, resid, scale, tm=512, tn=1024):
    T, K = a.shape
    tm = min(tm, T)
    N = w.shape[1]
    tn = _tile(N, tn)
    k_lo = (K // MXU_K_TILE + 1) // 2 * MXU_K_TILE
    assert 0 < k_lo < K and (K - k_lo) % 16 == 0, (K, k_lo)
    return pl.pallas_call(
        functools.partial(_ffn_out_kernel, scale=scale),
        grid=(N // tn, T // tm),
        in_specs=[pl.BlockSpec((tm, K), lambda j, i: (i, 0)),
                  pl.BlockSpec((k_lo, tn), lambda j, i: (0, j), pipeline_mode=pl.Buffered(1)),
                  pl.BlockSpec((pl.Element(K - k_lo), pl.Element(tn)), lambda j, i: (k_lo, j * tn),
                               pipeline_mode=pl.Buffered(1)),
                  pl.BlockSpec((tm, tn), lambda j, i: (i, j))],
        out_specs=[pl.BlockSpec((tm, tn), lambda j, i: (i, j)),
                   pl.BlockSpec((tm, tn), lambda j, i: (i, j)),
                   pl.BlockSpec((None, tm, LANES), lambda j, i: (j, i, 0))],
        out_shape=[jax.ShapeDtypeStruct((T, N), F32), jax.ShapeDtypeStruct((T, N), BF16),
                   jax.ShapeDtypeStruct((N // tn, T, LANES), F32)],
        compiler_params=_params("parallel", "parallel", vmem_limit=VMEM_LIMIT_WIDE_BYTES),
        name="ffn_out",
    )(a, w, w, resid)


def _mix_out_kernel(ya_ref, yb_ref, yc_ref, w_ref, r_ref, h_ref, hb_ref, ssq_ref, rsb_ref, rsc_ref):
    j = pl.program_id(1)
    ca, cb = ya_ref.shape[1], yb_ref.shape[1]

    @pl.when(j == 0)
    def _():
        def scale(y_ref):
            y = y_ref[...].astype(F32)
            return lax.rsqrt(jnp.mean(y * y, axis=-1, keepdims=True) + EPS)
        rsb_ref[...] = scale(yb_ref)
        rsc_ref[...] = scale(yc_ref)

    acc = jnp.dot(ya_ref[...], w_ref[0:ca, :], preferred_element_type=F32)
    acc = acc + rsb_ref[...] * jnp.dot(yb_ref[...], w_ref[ca:ca + cb, :], preferred_element_type=F32)
    acc = acc + rsc_ref[...] * jnp.dot(yc_ref[...], w_ref[ca + cb:, :], preferred_element_type=F32)
    _emit_stream(r_ref[...] + acc, j, h_ref, hb_ref, ssq_ref)


def mix_out(ya, yb, yc, w, resid, tm=1024, tn=512):
    T, ca = ya.shape
    tm = min(tm, T)
    cb, cc = yb.shape[1], yc.shape[1]
    K, N = w.shape
    tn = _tile(N, tn)
    out_specs, out_shape = _stream_out(T, N, tm, tn)
    rows = lambda c: pl.BlockSpec((tm, c), lambda i, j: (i, 0))
    return pl.pallas_call(
        _mix_out_kernel,
        grid=(T // tm, N // tn),
        in_specs=[rows(ca), rows(cb), rows(cc),
                  pl.BlockSpec((K, tn), lambda i, j: (0, j)),
                  pl.BlockSpec((tm, tn), lambda i, j: (i, j))],
        out_specs=out_specs,
        out_shape=out_shape,
        scratch_shapes=[pltpu.VMEM((tm, 1), F32), pltpu.VMEM((tm, 1), F32)],
        compiler_params=_params("parallel", "arbitrary", vmem_limit=VMEM_LIMIT_WIDE_BYTES),
        name="mix_out",
    )(ya, yb, yc, w, resid)


def _mix_prep_kernel(wt_ref, g_ref, o_ref, *, q_lo, q_hi, q_scale, transpose):
    i = pl.program_id(1)
    scale = jnp.where((i >= q_lo) & (i < q_hi), q_scale, 1.0).astype(F32)
    w = wt_ref[...] * (g_ref[...] * scale)
    o_ref[...] = (w.T if transpose else w).astype(o_ref.dtype)


def mix_prep(w_t, gain, row0, n_rows, q_rows, q_scale, transpose=True, tr=256):
    L, d_in, K = w_t.shape
    tr = _tile(n_rows, tr)
    assert d_in % SUBLANES == 0 and row0 % SUBLANES == 0 and q_rows[0] % tr == 0 and q_rows[1] % tr == 0
    assert row0 + n_rows <= d_in
    rows = lambda l, i: (pl.multiple_of(l * d_in + row0 + i * tr, SUBLANES), 0)
    if transpose:
        out_spec, out_dims = pl.BlockSpec((None, K, tr), lambda l, i: (l, 0, i)), (L, K, n_rows)
    else:
        out_spec, out_dims = pl.BlockSpec((None, tr, K), lambda l, i: (l, i, 0)), (L, n_rows, K)
    return pl.pallas_call(
        functools.partial(_mix_prep_kernel, q_lo=q_rows[0] // tr, q_hi=q_rows[1] // tr, q_scale=q_scale,
                          transpose=transpose),
        grid=(L, n_rows // tr),
        in_specs=[pl.BlockSpec((pl.Element(tr), pl.Element(K)), rows),
                  pl.BlockSpec((None, 1, K), lambda l, i: (l, 0, 0))],
        out_specs=out_spec,
        out_shape=jax.ShapeDtypeStruct(out_dims, BF16),
        compiler_params=_params("parallel", "parallel"),
        name="mix_prep",
    )(w_t.reshape(L * d_in, K), gain.astype(F32)[:, None, :])


def _scaled_matmul_kernel(a_ref, ssq_ref, w_ref, o_ref):
    rs = _row_scale(ssq_ref[...], a_ref.shape[1])
    o_ref[...] = (jnp.dot(a_ref[...], w_ref[...], preferred_element_type=F32) * rs).astype(o_ref.dtype)


def scaled_matmul(hb, ssq, w, layer, col0, N, out_dtype, tm=1024, tn=1024):
    T, K = hb.shape
    tm = min(tm, T)
    tn = _tile(math.gcd(N, col0), tn)
    j0 = col0 // tn
    return pl.pallas_call(
        _scaled_matmul_kernel,
        grid=(T // tm, N // tn),
        in_specs=[pl.BlockSpec((tm, K), lambda i, j: (i, 0)),
                  _ssq_spec(ssq, tm, lambda i, j: i),
                  pl.BlockSpec((None, K, tn), lambda i, j: (layer, 0, j0 + j))],
        out_specs=pl.BlockSpec((tm, tn), lambda i, j: (i, j)),
        out_shape=jax.ShapeDtypeStruct((T, N), out_dtype),
        compiler_params=_params("parallel", "arbitrary"),
        name="scaled_matmul",
    )(hb, ssq, w)


def _row_iota(shape):
    return lax.broadcasted_iota(jnp.int32, shape, 0)


def _log_sigmoid(x):
    return jnp.minimum(x, 0.0) - jnp.log1p(jnp.exp(-jnp.abs(x)))


def _softplus(x):
    return jnp.maximum(x, 0.0) + jnp.log1p(jnp.exp(-jnp.abs(x)))


def _forget_cumsum_kernel(x_ref, ssq_ref, w_ref, b_ref, o_ref, ot_ref, carry_ref):
    @pl.when(pl.program_id(1) == 0)
    def _():
        carry_ref[...] = jnp.zeros_like(carry_ref)

    rs = _row_scale(ssq_ref[...], x_ref.shape[1])
    z = _dot_nt(x_ref[...], w_ref[...]) * rs + b_ref[...]
    c = _log_sigmoid(z)
    ts = c.shape[0]
    rows = _row_iota(c.shape)
    shift = 1
    while shift < ts:
        c = c + jnp.where(rows >= shift, pltpu.roll(c, shift, 0), 0.0)
        shift *= 2
    c = c + carry_ref[...]
    o_ref[...] = c
    ot_ref[...] = c.T
    carry_ref[...] = c[ts - 1:ts, :]


def forget_cumsum(hb, ssq, w_ff, b_ff, layer, batch, ts=512):
    T, K = hb.shape
    S = T // batch
    ts = min(ts, S)
    nc = S // ts
    P = w_ff.shape[1]
    return pl.pallas_call(
        _forget_cumsum_kernel,
        grid=(batch, nc),
        in_specs=[pl.BlockSpec((ts, K), lambda b, c: (b * nc + c, 0)),
                  _ssq_spec(ssq, ts, lambda b, c: b * nc + c),
                  pl.BlockSpec((None, P, K), lambda b, c: (layer, 0, 0)),
                  pl.BlockSpec((None, 1, P), lambda b, c: (layer, 0, 0))],
        out_specs=[pl.BlockSpec((ts, P), lambda b, c: (b * nc + c, 0)),
                   pl.BlockSpec((None, P, ts), lambda b, c: (b, 0, c))],
        out_shape=[jax.ShapeDtypeStruct((T, P), F32), jax.ShapeDtypeStruct((batch, P, S), F32)],
        scratch_shapes=[pltpu.VMEM((1, P), F32)],
        compiler_params=_params("parallel", "arbitrary"),
        name="forget_cumsum",
    )(hb, ssq, w_ff, b_ff)


def _lru_kernel(xa_ref, ga_ref, cw_ref, cb_ref, wa_ref, ba_ref, wx_ref, bx_ref, lam_ref, gn_ref,
                o_ref, tail_ref, h_ref):
    @pl.when(pl.program_id(1) == 0)
    def _():
        tail_ref[...] = jnp.zeros_like(tail_ref)
        h_ref[...] = jnp.zeros_like(h_ref)

    x = xa_ref[...]
    ts, C = x.shape
    xb = jnp.concatenate([tail_ref[...], x], axis=0)
    y = cb_ref[...] + cw_ref[CONV_WIDTH - 1:CONV_WIDTH, :] * x
    for back in range(1, CONV_WIDTH):
        y = y + cw_ref[CONV_WIDTH - 1 - back:CONV_WIDTH - back, :] * pltpu.roll(xb, back, 0)[SUBLANES:, :]
    tail_ref[...] = x[ts - SUBLANES:, :]

    rs, gs = [], []
    for g in range(C // LRU_BLOCK):
        yg = y[:, g * LRU_BLOCK:(g + 1) * LRU_BLOCK].astype(BF16)
        rs.append(jnp.dot(yg, wa_ref[g], preferred_element_type=F32))
        gs.append(jnp.dot(yg, wx_ref[g], preferred_element_type=F32))
    r = jax.nn.sigmoid(jnp.concatenate(rs, axis=1) + ba_ref[...])
    i = jax.nn.sigmoid(jnp.concatenate(gs, axis=1) + bx_ref[...])

    log_a = (-LRU_C) * r * _softplus(-lam_ref[...])
    a = jnp.exp(log_a)
    th = jnp.abs(jnp.tanh(log_a))
    u = jnp.sqrt(2.0 * th / (1.0 + th)) * (i * y)

    n_groups = ts // SUBLANES
    a = a.reshape(n_groups, SUBLANES, C)
    u = u.reshape(n_groups, SUBLANES, C)
    sub = lax.broadcasted_iota(jnp.int32, a.shape, 1)
    shift = 1
    while shift < SUBLANES:
        keep = sub >= shift
        a_prev = jnp.where(keep, pltpu.roll(a, shift, 1), 1.0)
        u_prev = jnp.where(keep, pltpu.roll(u, shift, 1), 0.0)
        u = a * u_prev + u
        a = a * a_prev
        shift *= 2
    carry = h_ref[...]
    groups = []
    for g in range(n_groups):
        hg = a[g] * carry + u[g]
        groups.append(hg)
        carry = hg[SUBLANES - 1:SUBLANES, :]
    h = jnp.concatenate(groups, axis=0)
    h_ref[...] = carry

    out = h * jax.nn.gelu(ga_ref[...])
    ms = jnp.mean(out * out, axis=-1, keepdims=True)
    o_ref[...] = (out * lax.rsqrt(ms + EPS) * gn_ref[...]).astype(o_ref.dtype)


def lru_branch(proj, conv_w, conv_b, w_a, b_a, w_x, b_x, lam, gain, batch, ts=512):
    T = proj.shape[0]
    C = proj.shape[1] // 2
    S = T // batch
    ts = min(ts, S)
    nc = S // ts
    row = lambda v: v.reshape(1, C).astype(F32)
    full2 = lambda shape: pl.BlockSpec(shape, lambda b, c: (0, 0))
    full3 = lambda shape: pl.BlockSpec(shape, lambda b, c: (0, 0, 0))
    return pl.pallas_call(
        _lru_kernel,
        grid=(batch, nc),
        in_specs=[pl.BlockSpec((ts, C), lambda b, c: (b * nc + c, 0)),
                  pl.BlockSpec((ts, C), lambda b, c: (b * nc + c, 1)),
                  full2((CONV_WIDTH, C)), full2((1, C)),
                  full3(w_a.shape), full2((1, C)),
                  full3(w_x.shape), full2((1, C)),
                  full2((1, C)), full2((1, C))],
        out_specs=pl.BlockSpec((ts, C), lambda b, c: (b * nc + c, 0)),
        out_shape=jax.ShapeDtypeStruct((T, C), BF16),
        scratch_shapes=[pltpu.VMEM((SUBLANES, C), F32), pltpu.VMEM((1, C), F32)],
        compiler_params=_params("parallel", "arbitrary"),
        name="lru_branch",
    )(proj, proj, conv_w.astype(F32), row(conv_b), w_a.astype(BF16), row(b_a),
      w_x.astype(BF16), row(b_x), row(lam), row(gain))


def _softmax2(z_chunks, row_offset, p_dtype):
    m = z_chunks[0]
    for zc in z_chunks[1:]:
        m = jnp.maximum(m, zc)
    m = jnp.max(m, axis=-1, keepdims=True)
    shift = m if row_offset is None else (m + row_offset) - row_offset
    ps = [jnp.exp2(zc - shift) for zc in z_chunks]
    l = ps[0]
    for p in ps[1:]:
        l = l + p
    l = jnp.sum(l, axis=-1, keepdims=True)
    p = jnp.concatenate(ps, axis=1) if len(ps) > 1 else ps[0]
    return p.astype(p_dtype), l


def _causal_sweep(heads, tq):
    nq = heads[0][0].shape[0] // tq
    items = [(i, head) for i in reversed(range(nq)) for head in heads]

    def scores(i, head):
        q_ref, k_ref = head[0], head[1]
        return _dot_nt(q_ref[i * tq:(i + 1) * tq, :], k_ref[0:(i + 1) * tq, :])

    def probs(i, head, s):
        v_ref, bias_chunks, row_offset = head[2], head[4], head[5]
        z = [bias_chunks(i, c, s[:, c * tq:(c + 1) * tq]) for c in range(i + 1)]
        return _softmax2(z, None if row_offset is None else row_offset(i), v_ref.dtype)

    def values(i, head, p, l):
        v_ref, o_ref = head[2], head[3]
        pv = jnp.dot(p, v_ref[0:(i + 1) * tq, :], preferred_element_type=F32)
        o_ref[i * tq:(i + 1) * tq, :] = (pv / l).astype(o_ref.dtype)

    s_next, pending = scores(*items[0]), None
    for n, item in enumerate(items):
        s = s_next
        if n + 1 < len(items):
            s_next = scores(*items[n + 1])
        p, l = probs(*item, s)
        if pending is not None:
            values(*pending)
        pending = item + (p, l)
    values(*pending)


def _fox_kernel(q_ref, k_ref, v_ref, cum_ref, cumt_ref, tri_ref, o_ref, *, tq):
    n_group = q_ref.shape[1] // HEAD_DIM
    lane = lax.broadcasted_iota(jnp.int32, (tq, cum_ref.shape[1]), 1)
    heads = []
    for h in range(n_group):
        cols = pl.ds(h * HEAD_DIM, HEAD_DIM)
        head = pl.program_id(1) * n_group + h
        ck2 = cumt_ref[pl.ds(head, 1), :] * LOG2E

        def bias_chunks(i, c, s, ck2=ck2):
            z = s - ck2[:, c * tq:(c + 1) * tq]
            return z + tri_ref[...] if c == i else z

        def row_offset(i, head=head):
            cum_t = jnp.where(lane == head, cum_ref[i * tq:(i + 1) * tq, :], 0.0)
            return jnp.sum(cum_t, axis=-1, keepdims=True) * LOG2E

        heads.append((q_ref.at[:, cols], k_ref.at[:, cols], v_ref.at[:, cols], o_ref.at[:, cols],
                      bias_chunks, row_offset))
    _causal_sweep(heads, tq)


def fox_attention(qkv, col0, n_heads, cum, cum_t, batch, tq=256):
    T = qkv.shape[0]
    S = T // batch
    tq = min(tq, S)
    hp = math.gcd(HEADS_PER_STEP, n_heads)
    w = hp * HEAD_DIM
    tri = np.where(np.arange(tq)[:, None] >= np.arange(tq)[None, :], 0.0, NEG_BIG).astype(np.float32)
    assert col0 % hp == 0
    qkv_spec = lambda off: pl.BlockSpec((S, w), lambda b, g: (b, (col0 + off) // hp + g))
    return pl.pallas_call(
        functools.partial(_fox_kernel, tq=tq),
        grid=(batch, n_heads // hp),
        in_specs=[qkv_spec(0), qkv_spec(n_heads), qkv_spec(2 * n_heads),
                  pl.BlockSpec((S, cum.shape[1]), lambda b, g: (b, 0)),
                  pl.BlockSpec((None, cum_t.shape[1], S), lambda b, g: (b, 0, 0)),
                  pl.BlockSpec((tq, tq), lambda b, g: (0, 0))],
        out_specs=pl.BlockSpec((S, w), lambda b, g: (b, g)),
        out_shape=jax.ShapeDtypeStruct((T, n_heads * HEAD_DIM), BF16),
        compiler_params=_params("parallel", "parallel"),
        name="fox_attention",
    )(qkv, qkv, qkv, cum, cum_t, jnp.asarray(tri))


def _dilated_kernel(q_ref, k_ref, v_ref, rc_ref, ra_ref, rb_ref, lm_ref, wo_ref, gw_ref, o_ref, wob_ref, qs_ref,
                    ks_ref, *, tq):
    def rope(x):
        x = x.astype(F32)
        half = ROPE_DIM // 2
        return (x * rc_ref[...] + pltpu.roll(x, HEAD_DIM - half, 1) * ra_ref[...]
                + pltpu.roll(x, half, 1) * rb_ref[...])

    heads = [pl.ds(h * HEAD_DIM, HEAD_DIM) for h in range(q_ref.shape[1] // HEAD_DIM)]
    for cols in heads:
        qs_ref[:, cols] = rope(q_ref[:, cols]).astype(qs_ref.dtype)
        ks_ref[:, cols] = rope(k_ref[:, cols]).astype(ks_ref.dtype)
    add_log_multiplicity = lambda i, c, s: s + lm_ref[i - c]
    _causal_sweep([(qs_ref.at[:, cols], ks_ref.at[:, cols], v_ref.at[:, cols], o_ref.at[:, cols],
                    add_log_multiplicity, None) for cols in heads], tq)
    wob_ref[...] = (wo_ref[...] * gw_ref[...]).astype(wob_ref.dtype)


def _dilated_log2_multiplicity(S, tq):
    nq = S // tq
    delta = (np.arange(nq)[:, None, None] * tq + np.arange(tq)[None, :, None]
             - np.arange(tq)[None, None, :])
    mult = np.zeros(delta.shape, np.float64)
    for window, dilation in DILATED_PATTERNS:
        mult += (delta >= 0) & (delta <= window) & (delta % dilation == 0)
    with np.errstate(divide="ignore"):
        return np.where(mult > 0, np.log2(mult), NEG_BIG).astype(np.float32)


def _rope_tables(S):
    inv = 1.0 / (ROPE_THETA ** (jnp.arange(0, ROPE_DIM, 2, dtype=F32) / ROPE_DIM))
    ang = jnp.arange(S, dtype=F32)[:, None] * inv[None, :]
    cos, sin = jnp.cos(ang), jnp.sin(ang)
    half = ROPE_DIM // 2
    pad = lambda t, lo: jnp.concatenate(
        [jnp.zeros((S, lo), F32), t, jnp.zeros((S, HEAD_DIM - lo - t.shape[1]), F32)], axis=1)
    rc = jnp.concatenate([cos, cos, jnp.ones((S, HEAD_DIM - ROPE_DIM), F32)], axis=1)
    ra = pad(-sin, 0)
    rb = pad(sin, half)
    return rc, ra, rb


def dilated_attention(qkv, col0, n_heads, batch, w_out, w_gain, layer, tq=256):
    T = qkv.shape[0]
    S = T // batch
    tq = min(tq, S)
    hd = HEAD_DIM
    nq = S // tq
    hp = math.gcd(HEADS_PER_STEP, n_heads)
    w = hp * hd
    assert col0 % hp == 0
    ng = n_heads // hp
    Kw, Nw = w_out.shape[1], w_out.shape[2]
    slab = Kw // (batch * ng)
    assert Kw % (batch * ng) == 0 and slab % 16 == 0, (Kw, batch, ng)
    rc, ra, rb = _rope_tables(S)
    lm = jnp.asarray(_dilated_log2_multiplicity(S, tq))
    qkv_spec = lambda off: pl.BlockSpec((S, w), lambda b, g: (b, (col0 + off) // hp + g))
    tab = pl.BlockSpec((S, hd), lambda b, g: (0, 0))
    return pl.pallas_call(
        functools.partial(_dilated_kernel, tq=tq),
        grid=(batch, n_heads // hp),
        in_specs=[qkv_spec(0), qkv_spec(n_heads), qkv_spec(2 * n_heads), tab, tab, tab,
                  pl.BlockSpec((nq, tq, tq), lambda b, g: (0, 0, 0)),
                  pl.BlockSpec((None, slab, Nw), lambda b, g: (layer, b * ng + g, 0)),
                  pl.BlockSpec((slab, 1), lambda b, g: (b * ng + g, 0))],
        out_specs=[pl.BlockSpec((S, w), lambda b, g: (b, g)),
                   pl.BlockSpec((slab, Nw), lambda b, g: (b * ng + g, 0))],
        out_shape=[jax.ShapeDtypeStruct((T, n_heads * hd), BF16), jax.ShapeDtypeStruct((Kw, Nw), BF16)],
        scratch_shapes=[pltpu.VMEM((S, w), BF16), pltpu.VMEM((S, w), BF16)],
        compiler_params=_params("parallel", "parallel"),
        name="dilated_attention",
    )(qkv, qkv, qkv, rc, ra, rb, lm, w_out, w_gain.astype(F32).reshape(Kw, 1))


def kernel(x, ffn1_norm, ffn1_w_in, ffn1_w_out, mix_norm, mix_w_in, conv_w, conv_b, lru_w_a, lru_b_a,
           lru_w_x, lru_b_x, lru_lam, fox_b_f, out_norm_lru, out_norm_fox, out_norm_dil, mix_w_out,
           ffn2_norm, ffn2_w_in, ffn2_w_out, final_norm):
    B, S, D = x.shape
    T = B * S
    depth = ffn1_norm.shape[0]
    d_lru = conv_w.shape[2]
    n_fox = fox_b_f.shape[1]
    d_fox = n_fox * HEAD_DIM
    d_dil = out_norm_dil.shape[1]
    n_dil = d_dil // HEAD_DIM

    w_t = jnp.swapaxes(mix_w_in, 1, 2)
    q_scale = HEAD_DIM ** -0.5 * LOG2E
    ff0 = 2 * d_lru + 3 * d_fox
    w_head = mix_prep(w_t, mix_norm, 0, ff0, (2 * d_lru, 2 * d_lru + d_fox), q_scale)
    w_dil = mix_prep(w_t, mix_norm, ff0 + n_fox, 3 * d_dil, (0, d_dil), q_scale)
    w_ff = mix_prep(w_t, mix_norm, ff0, LANES, (0, 0), q_scale, transpose=False)
    b_ff = jnp.pad(fox_b_f.astype(F32), ((0, 0), (0, LANES - n_fox))).reshape(depth, 1, LANES)

    h = x.reshape(T, D)
    hb, ssq = rowstat(h)
    for l in range(depth):
        act, w_out_b = swiglu_in(hb, ssq, ffn1_norm[l], ffn1_w_in, ffn1_w_out, l)
        h, hb, ssq = ffn_out(act, w_out_b, h, 0.5)

        proj_lru = scaled_matmul(hb, ssq, w_head, l, 0, 2 * d_lru, F32)
        proj_fox = scaled_matmul(hb, ssq, w_head, l, 2 * d_lru, 3 * d_fox, BF16)
        proj_dil = scaled_matmul(hb, ssq, w_dil, l, 0, 3 * d_dil, BF16)
        cum, cum_t = forget_cumsum(hb, ssq, w_ff, b_ff, l, B)
        y_a = lru_branch(proj_lru, conv_w[l], conv_b[l], lru_w_a[l], lru_b_a[l], lru_w_x[l], lru_b_x[l],
                         lru_lam[l], out_norm_lru[l], B)
        y_b = fox_attention(proj_fox, 0, n_fox, cum, cum_t, B)
        w_gain = jnp.concatenate([jnp.ones((d_lru,), F32), out_norm_fox[l].astype(F32), out_norm_dil[l].astype(F32)])
        y_c, mix_w_out_b = dilated_attention(proj_dil, 0, n_dil, B, mix_w_out, w_gain, l)
        h, hb, ssq = mix_out(y_a, y_b, y_c, mix_w_out_b, h)

        act, w_out_b = swiglu_in(hb, ssq, ffn2_norm[l], ffn2_w_in, ffn2_w_out, l)
        h, hb, ssq = ffn_out(act, w_out_b, h, 0.5)
    return scale_norm(h, ssq, final_norm).reshape(B, S, D)
```
